```python
import jax, jax.numpy as jnp
from jax import lax
import numpy as np

D_MODEL = 1024
BATCH = 8
SEQ = 2048
DEPTH = 2
DEC_BATCH = 128
DEC_SEQ = 4
PAST_LEN = 16384
PAGE_SIZE = 128

D_MIX = D_MODEL
D_RWKV = D_MIX // 2
HEAD_DIM = 64
N_HEADS_R = D_RWKV // HEAD_DIM
D_CONV = D_MIX - D_RWKV
N_CONV_GROUPS = 8
CONV_W = 31
LORA_W = 64
LORA_A = 64
P_RWKV = 4 * D_RWKV + LORA_W + LORA_A
P_CONV = 3 * D_CONV
P_IN = P_RWKV + P_CONV
RMS_EPS = 1e-6
RWKV_GN_EPS = 64e-5
CONV_GN_EPS = 1e-5

kernel_name = "hymba_rwkv7_conformer_conv_decode_step"


def _rms_norm(x, g):
    xf = x.astype(jnp.float32)
    y = xf * lax.rsqrt(jnp.mean(jnp.square(xf), axis=-1, keepdims=True) + RMS_EPS)
    return (y * g.astype(jnp.float32)).astype(x.dtype)


def _group_norm(x, n_groups, g, b, eps):
    xf = x.astype(jnp.float32)
    xg = xf.reshape(xf.shape[:-1] + (n_groups, xf.shape[-1] // n_groups))
    mean = jnp.mean(xg, axis=-1, keepdims=True)
    var = jnp.mean(jnp.square(xg - mean), axis=-1, keepdims=True)
    y = ((xg - mean) * lax.rsqrt(var + eps)).reshape(xf.shape)
    return y * g.astype(jnp.float32) + b.astype(jnp.float32)


def _heads(t):
    return t.reshape(t.shape[:-1] + (N_HEADS_R, HEAD_DIM))


def _wkv7(S0, r, decay, k, v, kk, a):
    def step(S, inp):
        r_t, w_t, k_t, v_t, kk_t, a_t = inp
        Sk = jnp.einsum('bhvk,bhk->bhv', S, kk_t)
        S = (S * w_t[:, :, None, :]
             - Sk[..., None] * (kk_t * a_t)[:, :, None, :]
             + v_t[..., None] * k_t[:, :, None, :])
        y_t = jnp.einsum('bhvk,bhk->bhv', S, r_t)
        return S, y_t
    xs = tuple(jnp.moveaxis(t, 1, 0) for t in (r, decay, k, v, kk, a))
    S, y = lax.scan(step, S0, xs)
    return jnp.moveaxis(y, 0, 1), S


def _layer(x, c, s_shift, s_wkv, s_conv,
           w_ada, b_ada, g_pre, g_post, w_in, mu, w0, w_up, a0, a_up, k_k, k_a, r_k,
           gn_r_g, gn_r_b, w_dw, b_dw, gn_c_g, gn_c_b, w_out):
    dt = x.dtype
    B, T, _ = x.shape
    mod = jax.nn.silu(c) @ w_ada + b_ada
    shift, scale, gate = jnp.split(mod, 3, axis=-1)
    h = _rms_norm(x, g_pre) * (1 + scale[:, None]) + shift[:, None]
    u = h @ w_in
    u_r, u_c = u[..., :P_RWKV], u[..., P_RWKV:]

    u_prev0 = s_shift.astype(dt) @ w_in[:, :P_RWKV]
    u_prev = jnp.concatenate([u_prev0[:, None], u_r[:, :-1]], axis=1)
    u_r = u_r + (u_prev - u_r) * mu
    r = u_r[..., 0 * D_RWKV:1 * D_RWKV]
    k = u_r[..., 1 * D_RWKV:2 * D_RWKV]
    v = u_r[..., 2 * D_RWKV:3 * D_RWKV]
    g_r = u_r[..., 3 * D_RWKV:4 * D_RWKV]
    w_lo = u_r[..., 4 * D_RWKV:4 * D_RWKV + LORA_W]
    a_lo = u_r[..., 4 * D_RWKV + LORA_W:]
    w_log = -jax.nn.softplus(-(w0 + jnp.tanh(w_lo) @ w_up).astype(jnp.float32)) - 0.5
    decay = jnp.exp(-jnp.exp(w_log))
    a = jax.nn.sigmoid((a0 + a_lo @ a_up).astype(jnp.float32))
    kf = k.astype(jnp.float32)
    kk = _heads(kf * k_k.astype(jnp.float32))
    kk = kk / jnp.maximum(jnp.sqrt(jnp.sum(jnp.square(kk), axis=-1, keepdims=True)), 1e-12)
    k_mod = kf * (1 + (a - 1) * k_a.astype(jnp.float32))
    rh = _heads(r.astype(jnp.float32))
    kh = _heads(k_mod)
    vh = _heads(v.astype(jnp.float32))
    y, S_new = _wkv7(s_wkv.astype(jnp.float32), rh, _heads(decay), kh, vh, kk, _heads(a))
    y = _group_norm(y.reshape(B, T, D_RWKV), N_HEADS_R, gn_r_g, gn_r_b, RWKV_GN_EPS)
    bonus = jnp.sum(rh * kh * r_k.astype(jnp.float32), axis=-1, keepdims=True) * vh
    y_r = (y + bonus.reshape(B, T, D_RWKV)) * jax.nn.silu(g_r.astype(jnp.float32))

    glu_a = u_c[..., :D_CONV]
    glu_b = u_c[..., D_CONV:2 * D_CONV]
    g_c = u_c[..., 2 * D_CONV:]
    glu = glu_a * jax.nn.sigmoid(glu_b)
    buf = jnp.concatenate([s_conv.astype(dt), glu], axis=1)
    conv = lax.conv_general_dilated(
        buf, w_dw[:, None, :], window_strides=(1,), padding='VALID',
        dimension_numbers=('NWC', 'WIO', 'NWC'), feature_group_count=D_CONV) + b_dw
    y_c = jax.nn.silu(_group_norm(conv, N_CONV_GROUPS, gn_c_g, gn_c_b, CONV_GN_EPS)) \
        * jax.nn.silu(g_c.astype(jnp.float32))

    mix = jnp.concatenate([y_r, y_c], axis=-1).astype(dt) @ w_out
    x = x + gate[:, None] * _rms_norm(mix, g_post)
    return x, h[:, -1], S_new, buf[:, -(CONV_W - 1):]


def _trunk(x, c, st_shift, st_wkv, st_conv, layer_w):
    shifts, wkvs, convs = [], [], []
    for l in range(DEPTH):
        x, s1, s2, s3 = _layer(x, c, st_shift[l], st_wkv[l], st_conv[l],
                               *[p[l] for p in layer_w])
        shifts.append(s1)
        wkvs.append(s2)
        convs.append(s3)
    return x, jnp.stack(shifts), jnp.stack(wkvs), jnp.stack(convs)


def setup_inputs(seed: int = 0) -> dict:
    key = jax.random.key(seed)
    ks = jax.random.split(key, 32)
    f = jnp.float32
    nrm = lambda k, shape, s: jax.random.normal(k, shape, f) * s
    return {
        "x_prompt": nrm(ks[0], (BATCH, SEQ, D_MODEL), 1.0),
        "x_sample": nrm(ks[1], (DEC_BATCH, DEC_SEQ, D_MODEL), 1.0),
        "c_prompt": nrm(ks[2], (BATCH, D_MODEL), 1.0),
        "c_sample": nrm(ks[3], (DEC_BATCH, D_MODEL), 1.0),
        "state_shift": nrm(ks[4], (DEPTH, DEC_BATCH, D_MODEL), 1.0),
        "state_wkv": nrm(ks[5], (DEPTH, DEC_BATCH, N_HEADS_R, HEAD_DIM, HEAD_DIM), 0.1),
        "state_conv": nrm(ks[6], (DEPTH, DEC_BATCH, CONV_W - 1, D_CONV), 0.5),
        "w_ada": nrm(ks[7], (DEPTH, D_MODEL, 3 * D_MODEL), 0.5 * D_MODEL ** -0.5),
        "b_ada": nrm(ks[8], (DEPTH, 3 * D_MODEL), 0.01),
        "g_pre": 1.0 + nrm(ks[9], (DEPTH, D_MODEL), 0.01),
        "g_post": 1.0 + nrm(ks[10], (DEPTH, D_MODEL), 0.01),
        "w_in": nrm(ks[11], (DEPTH, D_MODEL, P_IN), D_MODEL ** -0.5),
        "mu": jax.random.uniform(ks[12], (DEPTH, P_RWKV), f),
        "w0": jax.random.uniform(ks[13], (DEPTH, D_RWKV), f, -4.0, 1.0),
        "w_up": nrm(ks[14], (DEPTH, LORA_W, D_RWKV), 0.5 * LORA_W ** -0.5),
        "a0": nrm(ks[15], (DEPTH, D_RWKV), 0.1),
        "a_up": nrm(ks[16], (DEPTH, LORA_A, D_RWKV), LORA_A ** -0.5),
        "k_k": 0.85 + nrm(ks[17], (DEPTH, D_RWKV), 0.02),
        "k_a": 1.0 + nrm(ks[18], (DEPTH, D_RWKV), 0.02),
        "r_k": nrm(ks[19], (DEPTH, N_HEADS_R, HEAD_DIM), 0.1),
        "gn_r_g": 1.0 + nrm(ks[20], (DEPTH, D_RWKV), 0.01),
        "gn_r_b": nrm(ks[21], (DEPTH, D_RWKV), 0.01),
        "w_dw": nrm(ks[22], (DEPTH, CONV_W, D_CONV), CONV_W ** -0.5),
        "b_dw": nrm(ks[23], (DEPTH, D_CONV), 0.01),
        "gn_c_g": 1.0 + nrm(ks[24], (DEPTH, D_CONV), 0.01),
        "gn_c_b": nrm(ks[25], (DEPTH, D_CONV), 0.01),
        "w_out": nrm(ks[26], (DEPTH, D_MIX, D_MODEL), D_MIX ** -0.5),
    }


def reference(x_prompt, x_sample, c_prompt, c_sample, state_shift, state_wkv, state_conv,
              w_ada, b_ada, g_pre, g_post, w_in, mu, w0, w_up, a0, a_up, k_k, k_a, r_k,
              gn_r_g, gn_r_b, w_dw, b_dw, gn_c_g, gn_c_b, w_out):
    layer_w = (w_ada, b_ada, g_pre, g_post, w_in, mu, w0, w_up, a0, a_up, k_k, k_a, r_k,
               gn_r_g, gn_r_b, w_dw, b_dw, gn_c_g, gn_c_b, w_out)
    B = x_prompt.shape[0]
    z_shift = jnp.zeros((DEPTH, B, D_MODEL), x_prompt.dtype)
    z_wkv = jnp.zeros((DEPTH, B, N_HEADS_R, HEAD_DIM, HEAD_DIM), jnp.float32)
    z_conv = jnp.zeros((DEPTH, B, CONV_W - 1, D_CONV), x_prompt.dtype)
    y_prompt, shift_p, wkv_p, conv_p = _trunk(x_prompt, c_prompt, z_shift, z_wkv, z_conv, layer_w)
    y_sample, shift_s, wkv_s, conv_s = _trunk(x_sample, c_sample, state_shift, state_wkv,
                                              state_conv, layer_w)
    return (y_prompt, y_sample, shift_p, wkv_p, conv_p, shift_s, wkv_s, conv_s)
```

```python
import functools

import jax
import jax.numpy as jnp
from jax import lax
from jax.experimental import pallas as pl
from jax.experimental.pallas import tpu as pltpu

F32 = jnp.float32
BF16 = jnp.bfloat16

D_MODEL = 1024
D_RWKV = 512
D_CONV = 512
HEAD_DIM = 64
N_HEADS = 8
N_PAIRS = N_HEADS // 2
LORA = 64
P_RWKV = 4 * D_RWKV + 2 * LORA
P_IN = P_RWKV + 3 * D_CONV
CONV_W = 31
HIST = CONV_W - 1
RMS_EPS = 1e-6
RWKV_GN_EPS = 64e-5
CONV_GN_EPS = 1e-5

LANES = 128
CHUNK = 64
VMEM_LIMIT = 56 * 1024 * 1024

_NN = (((1,), (0,)), ((), ()))
_NT = (((1,), (1,)), ((), ()))
_TN = (((0,), (0,)), ((), ()))


def _dg(a, b, dims=_NN):
    return lax.dot_general(a, b, dims, preferred_element_type=F32)


def _split2(x):
    hi = x.astype(BF16)
    lo = (x - hi.astype(F32)).astype(BF16)
    return hi, lo


def _mm(a, b, dims=_NN, passes=1):
    if passes == 1:
        return _dg(a.astype(BF16), b.astype(BF16), dims)
    ah, al = _split2(a)
    bh, bl = _split2(b)
    return _dg(ah, bh, dims) + (_dg(ah, bl, dims) + _dg(al, bh, dims))


def _segsum(x, bones):
    hi, lo = _split2(x)
    return _dg(hi, bones) + _dg(lo, bones)


def _sigmoid(x):
    return 1.0 / (1.0 + jnp.exp(-x))


def _group_norm(x, g, b, eps, bones):
    inv_n = 1.0 / HEAD_DIM
    mean = _segsum(x, bones) * inv_n
    d = x - mean
    var = _segsum(d * d, bones) * inv_n
    return d * lax.rsqrt(var + eps) * g + b


def _mod_body(c_ref, w_ref, b_ref, o_ref):
    c = c_ref[...]
    s = (c * _sigmoid(c)).astype(BF16)
    o_ref[0] = _dg(s, w_ref[0].astype(BF16)) + b_ref[0]


def _mod_call(c_all, w_ada, b_ada):
    depth, _, n3 = w_ada.shape
    rows = c_all.shape[0]
    bn = 768
    return pl.pallas_call(
        _mod_body,
        grid=(depth, n3 // bn),
        in_specs=[
            pl.BlockSpec((rows, D_MODEL), lambda l, j: (0, 0)),
            pl.BlockSpec((1, D_MODEL, bn), lambda l, j: (l, 0, j)),
            pl.BlockSpec((1, 1, bn), lambda l, j: (l, 0, j)),
        ],
        out_specs=pl.BlockSpec((1, rows, bn), lambda l, j: (l, 0, j)),
        out_shape=jax.ShapeDtypeStruct((depth, rows, n3), F32),
        compiler_params=pltpu.CompilerParams(
            dimension_semantics=("arbitrary", "arbitrary"), vmem_limit_bytes=VMEM_LIMIT),
        name="adaln_mod",
    )(c_all, w_ada, b_ada.reshape(depth, 1, n3))


def _inproj_body(time_major, x_ref, sh_ref, sc_ref, ss_ref, gpre_ref, win_ref, mu_ref,
                 w0_ref, a0_ref, wl_ref, kkp_ref, kap_ref, rkp_ref, bones_ref,
                 r_o, lw_o, km_o, v_o, kk_o, kka_o, bon_o, sgr_o, glu_o, sgc_o, hl_o, carry):
    step = pl.program_id(1)
    x = x_ref[0]
    rows = x.shape[0]
    ms = jnp.mean(x * x, axis=-1, keepdims=True)
    h = x * lax.rsqrt(ms + RMS_EPS) * gpre_ref[...]
    h = h * (1.0 + sc_ref[0]) + sh_ref[0]
    u = _dg(h.astype(BF16), win_ref[...])
    ur = u[:, :P_RWKV]

    if time_major:
        @pl.when(step == 0)
        def _():
            carry[...] = _dg(ss_ref[0].astype(BF16), win_ref[:, :P_RWKV])
        prev = carry[...]
        carry[...] = ur
        hl_o[0] = h
    else:
        @pl.when(step == 0)
        def _():
            ss8 = jnp.broadcast_to(ss_ref[0], (8, D_MODEL)).astype(BF16)
            carry[7:8, :] = _dg(ss8, win_ref[:, :P_RWKV])[0:1]
        carry[8:8 + rows, :] = ur
        prev = carry[7:7 + rows, :]
        carry[7:8, :] = ur[rows - 1:rows]
        hl_o[0] = h[rows - 1:rows]

    urs = ur + (prev - ur) * mu_ref[...]
    r = urs[:, 0:D_RWKV]
    k = urs[:, D_RWKV:2 * D_RWKV]
    v = urs[:, 2 * D_RWKV:3 * D_RWKV]
    g_r = urs[:, 3 * D_RWKV:4 * D_RWKV]
    lora = urs[:, 4 * D_RWKV:P_RWKV]
    lane = lax.broadcasted_iota(jnp.int32, lora.shape, 1)
    lora_in = jnp.where(lane < LORA, jnp.tanh(lora), lora).astype(BF16)
    wa = _dg(lora_in, wl_ref[...])
    z = -(w0_ref[...] + wa[:, :D_RWKV])
    softplus = jnp.maximum(z, 0.0) + jnp.log(1.0 + jnp.exp(-jnp.abs(z)))
    lw = -jnp.exp(-softplus - 0.5)
    a = _sigmoid(a0_ref[...] + wa[:, D_RWKV:])

    bones = bones_ref[...]
    kkr = k * kkp_ref[...]
    ssq = _segsum(kkr * kkr, bones)
    kk = kkr * lax.rsqrt(jnp.maximum(ssq, 1e-24))
    km = k * (1.0 + (a - 1.0) * kap_ref[...])
    bonus = _segsum(r * km * rkp_ref[...], bones) * v

    r_o[0] = r
    lw_o[0] = lw
    km_o[0] = km
    v_o[0] = v
    kk_o[0] = kk
    kka_o[0] = kk * a
    bon_o[0] = bonus
    sgr_o[0] = g_r * _sigmoid(g_r)

    glu_a = u[:, P_RWKV:P_RWKV + D_CONV]
    glu_b = u[:, P_RWKV + D_CONV:P_RWKV + 2 * D_CONV]
    g_c = u[:, P_RWKV + 2 * D_CONV:]
    glu_o[0] = glu_a * _sigmoid(glu_b)
    sgc_o[0] = g_c * _sigmoid(g_c)


def _inproj_call(time_major, x, shift, scale, s_shift, lw_params, bones, tile):
    n_outer, n_inner, _ = x.shape
    if time_major:
        grid = (1, n_outer)
        rows = n_inner
        x_map = lambda i, t: (t, 0, 0)
        row_map = lambda i, t: (0, 0, 0)
        row_block = (1, rows, D_MODEL)
        carry = pltpu.VMEM((rows, P_RWKV), F32)
        hl_shape = (1, rows, D_MODEL)
    else:
        rows = tile
        grid = (n_outer, n_inner // rows)
        x_map = lambda b, t: (b, t, 0)
        row_map = lambda b, t: (b, 0, 0)
        row_block = (1, 1, D_MODEL)
        carry = pltpu.VMEM((rows + 8, P_RWKV), F32)
        hl_shape = (n_outer, 1, D_MODEL)
    const = lambda i, t: (0, 0)
    g_pre, w_in, mu, w0, a0, w_lora, k_k, k_a, r_k = lw_params
    params = [g_pre, w_in, mu, w0, a0, w_lora, k_k, k_a, r_k, bones]
    out_block = pl.BlockSpec((1, rows, D_RWKV), x_map)
    out_sds = jax.ShapeDtypeStruct((n_outer, n_inner, D_RWKV), F32)
    return pl.pallas_call(
        functools.partial(_inproj_body, time_major),
        grid=grid,
        in_specs=[pl.BlockSpec((1, rows, D_MODEL), x_map)]
        + [pl.BlockSpec(row_block, row_map)] * 3
        + [pl.BlockSpec(p.shape, const) for p in params],
        out_specs=[out_block] * 10 + [pl.BlockSpec((1,) + hl_shape[1:], row_map)],
        out_shape=[out_sds] * 10 + [jax.ShapeDtypeStruct(hl_shape, F32)],
        scratch_shapes=[carry],
        compiler_params=pltpu.CompilerParams(
            dimension_semantics=("arbitrary", "arbitrary"), vmem_limit_bytes=VMEM_LIMIT),
        name="inproj_tm" if time_major else "inproj_bm",
    )(x, shift, scale, s_shift, *params)


PASS_GRAM = 3
PASS_INV = 3
PASS_STATE = 3
PASS_OUT = 3


def _wkv_chunk_body(r_ref, lw_ref, km_ref, v_ref, kk_ref, kka_ref, s0_ref, lt_ref,
                    y_ref, so_ref, s_scr):
    c = pl.program_id(1)
    n_chunks = pl.num_programs(1)
    zeros_hh = jnp.zeros((HEAD_DIM, HEAD_DIM), F32)

    @pl.when(c == 0)
    def _():
        for p in range(N_PAIRS):
            top = jnp.concatenate([s0_ref[0, 2 * p], zeros_hh], axis=1)
            bot = jnp.concatenate([zeros_hh, s0_ref[0, 2 * p + 1]], axis=1)
            s_scr[p] = jnp.concatenate([top, bot], axis=0)

    lw = lw_ref[0]
    l_hi = lw.astype(BF16)
    l_r1 = lw - l_hi.astype(F32)
    l_mid = l_r1.astype(BF16)
    l_lo = (l_r1 - l_mid.astype(F32)).astype(BF16)
    lt = lt_ref[...]
    cum = _dg(lt, l_hi) + (_dg(lt, l_mid) + _dg(lt, l_lo))
    cum_last = cum[CHUNK - 1:CHUNK, :]
    p_inc = jnp.exp(cum)
    p_exc = jnp.exp(cum - lw)
    p_inv = jnp.exp(-cum)
    p_end = jnp.exp(cum_last - cum)
    p_last = jnp.exp(cum_last)

    at_all = -kk_ref[0] * p_exc
    rt_all = r_ref[0] * p_inc
    kka = kka_ref[0]
    km = km_ref[0]
    bt_all = kka * p_inv
    kt_all = km * p_inv
    bh_all = kka * p_end
    kh_all = km * p_end
    v_all = v_ref[0]

    lane = lax.broadcasted_iota(jnp.int32, (CHUNK, LANES), 1)
    first = lane < HEAD_DIM
    row = lax.broadcasted_iota(jnp.int32, (LANES, LANES), 0)
    col = lax.broadcasted_iota(jnp.int32, (LANES, LANES), 1)
    same_head = (row < HEAD_DIM) == (col < HEAD_DIM)
    rt_i = row & (HEAD_DIM - 1)
    ct_i = col & (HEAD_DIM - 1)
    tri_strict = same_head & (ct_i < rt_i)
    tri_incl = same_head & (ct_i <= rt_i)
    eye = (row == col).astype(F32)

    def stack(xp):
        return jnp.concatenate([jnp.where(first, xp, 0.0), jnp.where(first, 0.0, xp)], axis=0)

    for p in range(N_PAIRS):
        sl = slice(LANES * p, LANES * (p + 1))
        x2 = jnp.concatenate([stack(at_all[:, sl]), stack(rt_all[:, sl])], axis=0)
        y2 = jnp.concatenate([stack(bt_all[:, sl]), stack(kt_all[:, sl])], axis=0)
        vs = stack(v_all[:, sl])
        hk = jnp.concatenate([stack(bh_all[:, sl]), stack(kh_all[:, sl])], axis=0)

        g = _mm(x2, y2, _NT, PASS_GRAM)
        n_mat = jnp.where(tri_strict, g[:LANES, :LANES], 0.0)
        a_ak = jnp.where(tri_strict, g[:LANES, LANES:], 0.0)
        a_rb = jnp.where(tri_incl, g[LANES:, :LANES], 0.0)
        a_rk = jnp.where(tri_incl, g[LANES:, LANES:], 0.0)

        t_inv = eye + n_mat
        n_pow = n_mat
        span = 2
        while span < CHUNK:
            n_pow = _mm(n_pow, n_pow, _NN, PASS_INV)
            t_inv = t_inv + _mm(t_inv, n_pow, _NN, PASS_INV)
            span *= 2

        s_bd = s_scr[p]
        xs = _mm(x2, s_bd, _NT, PASS_STATE)
        rhs = xs[:LANES] + _mm(a_ak, vs, _NN, PASS_STATE)
        z_mat = _mm(t_inv, rhs, _NN, PASS_STATE)
        zv = jnp.concatenate([z_mat, vs], axis=0)
        y_bd = xs[LANES:] + _mm(jnp.concatenate([a_rb, a_rk], axis=1), zv, _NN, PASS_OUT)
        y_ref[0, :, sl] = y_bd[:HEAD_DIM] + y_bd[HEAD_DIM:]
        s_new = s_bd * p_last[:, sl] + _mm(zv, hk, _TN, PASS_STATE)
        s_scr[p] = s_new

        @pl.when(c == n_chunks - 1)
        def _():
            so_ref[0, 2 * p] = s_new[:HEAD_DIM, :HEAD_DIM]
            so_ref[0, 2 * p + 1] = s_new[HEAD_DIM:, HEAD_DIM:]


def _wkv_chunk_call(r, lw, km, v, kk, kka, s0, lt):
    n_b, n_t, _ = r.shape
    seq_block = pl.BlockSpec((1, CHUNK, D_RWKV), lambda b, c: (b, c, 0))
    st_block = pl.BlockSpec((1, N_HEADS, HEAD_DIM, HEAD_DIM), lambda b, c: (b, 0, 0, 0))
    return pl.pallas_call(
        _wkv_chunk_body,
        grid=(n_b, n_t // CHUNK),
        in_specs=[seq_block] * 6 + [st_block, pl.BlockSpec(lt.shape, lambda b, c: (0, 0))],
        out_specs=[seq_block, st_block],
        out_shape=[jax.ShapeDtypeStruct(r.shape, F32), jax.ShapeDtypeStruct(s0.shape, F32)],
        scratch_shapes=[pltpu.VMEM((N_PAIRS, LANES, LANES), F32)],
        compiler_params=pltpu.CompilerParams(
            dimension_semantics=("arbitrary", "arbitrary"), vmem_limit_bytes=VMEM_LIMIT),
        name="wkv_chunk",
    )(r, lw, km, v, kk, kka, s0, lt)


SEQ_BATCH_BLOCK = 8


def _wkv_seq_body(r_ref, lw_ref, km_ref, v_ref, kk_ref, kka_ref, s_ref, y_ref, so_ref):
    n_t = r_ref.shape[0]
    lane = lax.broadcasted_iota(jnp.int32, (HEAD_DIM, LANES), 1)
    rowi = lax.broadcasted_iota(jnp.int32, (HEAD_DIM, LANES), 0)
    first = lane < HEAD_DIM
    eye2 = ((lane & (HEAD_DIM - 1)) == rowi).astype(F32)

    def headsum(xt):
        s0 = jnp.sum(jnp.where(first, xt, 0.0), axis=-1, keepdims=True)
        s1 = jnp.sum(jnp.where(first, 0.0, xt), axis=-1, keepdims=True)
        return jnp.where(first, s0, s1)

    for b in range(s_ref.shape[0]):
        for p in range(N_PAIRS):
            s = jnp.concatenate([s_ref[b, 2 * p], s_ref[b, 2 * p + 1]], axis=1)
            for t in range(n_t):
                idx = (t, slice(b, b + 1), slice(LANES * p, LANES * (p + 1)))
                kk = kk_ref[idx]
                sk = headsum(s * kk)
                vcol = headsum(eye2 * v_ref[idx])
                s = s * jnp.exp(lw_ref[idx]) - sk * kka_ref[idx] + vcol * km_ref[idx]
                yb = headsum(s * r_ref[idx])
                y_ref[idx] = jnp.sum(yb * eye2, axis=0, keepdims=True)
            so_ref[b, 2 * p] = s[:, :HEAD_DIM]
            so_ref[b, 2 * p + 1] = s[:, HEAD_DIM:]


def _wkv_seq_call(r, lw, km, v, kk, kka, s0):
    n_t, n_b, _ = r.shape
    bb = SEQ_BATCH_BLOCK
    seq_block = pl.BlockSpec((n_t, bb, D_RWKV), lambda i: (0, i, 0))
    st_block = pl.BlockSpec((bb, N_HEADS, HEAD_DIM, HEAD_DIM), lambda i: (i, 0, 0, 0))
    return pl.pallas_call(
        _wkv_seq_body,
        grid=(n_b // bb,),
        in_specs=[seq_block] * 6 + [st_block],
        out_specs=[seq_block, st_block],
        out_shape=[jax.ShapeDtypeStruct(r.shape, F32), jax.ShapeDtypeStruct(s0.shape, F32)],
        compiler_params=pltpu.CompilerParams(
            dimension_semantics=("arbitrary",), vmem_limit_bytes=VMEM_LIMIT),
        name="wkv_seq",
    )(r, lw, km, v, kk, kka, s0)


CONV_ROW_BLOCK = 32
HIST_PAD = 32


def _merge_and_residual(y, bonus, sgr, conv, sgc, x, gate, gnr_g, gnr_b, gnc_g, gnc_b,
                        wout_ref, gpost, bones):
    y_r = (_group_norm(y, gnr_g, gnr_b, RWKV_GN_EPS, bones) + bonus) * sgr
    cn = _group_norm(conv, gnc_g, gnc_b, CONV_GN_EPS, bones)
    y_c = cn * _sigmoid(cn) * sgc
    mix = _dg(y_r.astype(BF16), wout_ref[0:D_RWKV, :]) + _dg(y_c.astype(BF16), wout_ref[D_RWKV:, :])
    ms = jnp.mean(mix * mix, axis=-1, keepdims=True)
    return x + gate * (mix * lax.rsqrt(ms + RMS_EPS) * gpost)


def _post_bm_body(y_ref, bon_ref, sgr_ref, glu_ref, sgc_ref, x_ref, gate_ref, sconv_ref,
                  gnrg_ref, gnrb_ref, wdw_ref, bdw_ref, gncg_ref, gncb_ref, wout_ref, gpost_ref,
                  bones_ref, xo_ref, buf, conv):
    step = pl.program_id(1)
    rows = y_ref.shape[1]
    off = HIST_PAD - HIST

    @pl.when(step == 0)
    def _():
        buf[off:HIST_PAD, :] = sconv_ref[0]

    buf[HIST_PAD:HIST_PAD + rows, :] = glu_ref[0]
    for r0 in range(0, rows, CONV_ROW_BLOCK):
        acc = jnp.broadcast_to(bdw_ref[...], (CONV_ROW_BLOCK, D_CONV))
        for j in range(CONV_W):
            acc = acc + wdw_ref[j:j + 1, :] * buf[r0 + j + off:r0 + j + off + CONV_ROW_BLOCK, :]
        conv[r0:r0 + CONV_ROW_BLOCK, :] = acc
    buf[off:HIST_PAD, :] = buf[rows + off:rows + HIST_PAD, :]

    xo_ref[0] = _merge_and_residual(
        y_ref[0], bon_ref[0], sgr_ref[0], conv[...], sgc_ref[0], x_ref[0], gate_ref[0],
        gnrg_ref[...], gnrb_ref[...], gncg_ref[...], gncb_ref[...], wout_ref, gpost_ref[...],
        bones_ref[...])


def _post_tm_body(y_ref, bon_ref, sgr_ref, glu_ref, sgc_ref, x_ref, gate_ref, sconv_ref,
                  gnrg_ref, gnrb_ref, wdw_ref, bdw_ref, gncg_ref, gncb_ref, wout_ref, gpost_ref,
                  bones_ref, xo_ref):
    n_t, rows, _ = y_ref.shape
    for t in range(n_t):
        acc = jnp.broadcast_to(bdw_ref[...], (rows, D_CONV))
        for j in range(CONV_W):
            i = t + j
            src = sconv_ref[i] if i < HIST else glu_ref[i - HIST]
            acc = acc + wdw_ref[j:j + 1, :] * src
        xo_ref[t] = _merge_and_residual(
            y_ref[t], bon_ref[t], sgr_ref[t], acc, sgc_ref[t], x_ref[t], gate_ref[0],
            gnrg_ref[...], gnrb_ref[...], gncg_ref[...], gncb_ref[...], wout_ref, gpost_ref[...],
            bones_ref[...])


def _post_call(time_major, y, bonus, sgr, glu, sgc, x, gate, s_conv, post_params, bones, tile):
    n_outer, n_inner, _ = x.shape
    const = lambda i, t: (0, 0)
    if time_major:
        rows = n_inner
        grid = (1, 1)
        lead = n_outer
        x_map = lambda i, t: (0, 0, 0)
        gate_spec = pl.BlockSpec((1, rows, D_MODEL), lambda i, t: (0, 0, 0))
        sconv_spec = pl.BlockSpec(s_conv.shape, lambda i, t: (0, 0, 0))
        scratch = []
        body = _post_tm_body
    else:
        rows = tile
        grid = (n_outer, n_inner // rows)
        lead = 1
        x_map = lambda b, t: (b, t, 0)
        gate_spec = pl.BlockSpec((1, 1, D_MODEL), lambda b, t: (b, 0, 0))
        sconv_spec = pl.BlockSpec((1, HIST, D_CONV), lambda b, t: (b, 0, 0))
        scratch = [pltpu.VMEM((HIST_PAD + rows, D_CONV), F32), pltpu.VMEM((rows, D_CONV), F32)]
        body = _post_bm_body
    params = list(post_params) + [bones]
    half = pl.BlockSpec((lead, rows, D_RWKV), x_map)
    full = pl.BlockSpec((lead, rows, D_MODEL), x_map)
    return pl.pallas_call(
        body,
        grid=grid,
        in_specs=[half] * 5 + [full, gate_spec, sconv_spec]
        + [pl.BlockSpec(p.shape, const) for p in params],
        out_specs=full,
        out_shape=jax.ShapeDtypeStruct(x.shape, F32),
        scratch_shapes=scratch,
        compiler_params=pltpu.CompilerParams(
            dimension_semantics=("arbitrary", "arbitrary"), vmem_limit_bytes=VMEM_LIMIT),
        name="post_tm" if time_major else "post_bm",
    )(y, bonus, sgr, glu, sgc, x, gate, s_conv, *params)


PROMPT_TILE = 256


def _block_ones():
    i = lax.broadcasted_iota(jnp.int32, (D_RWKV, D_RWKV), 0) // HEAD_DIM
    j = lax.broadcasted_iota(jnp.int32, (D_RWKV, D_RWKV), 1) // HEAD_DIM
    return (i == j).astype(BF16)


def _lower_tri():
    i = lax.broadcasted_iota(jnp.int32, (CHUNK, CHUNK), 0)
    j = lax.broadcasted_iota(jnp.int32, (CHUNK, CHUNK), 1)
    return (j <= i).astype(BF16)


def kernel(x_prompt, x_sample, c_prompt, c_sample, state_shift, state_wkv, state_conv, w_ada, b_ada, g_pre, g_post, w_in, mu, w0, w_up, a0, a_up, k_k, k_a, r_k, gn_r_g, gn_r_b, w_dw, b_dw, gn_c_g, gn_c_b, w_out):
    depth = w_in.shape[0]
    n_p, t_p, _ = x_prompt.shape
    n_s, t_s, _ = x_sample.shape
    bones = _block_ones()
    lt = _lower_tri()

    mod = _mod_call(jnp.concatenate([c_prompt, c_sample], axis=0), w_ada, b_ada)

    xp = x_prompt
    xs = jnp.transpose(x_sample, (1, 0, 2))
    zeros_shift = jnp.zeros((n_p, 1, D_MODEL), F32)
    zeros_wkv = jnp.zeros((n_p, N_HEADS, HEAD_DIM, HEAD_DIM), F32)
    zeros_conv = jnp.zeros((n_p, HIST, D_CONV), F32)

    shifts_p, wkvs_p, convs_p, shifts_s, wkvs_s, convs_s = [], [], [], [], [], []
    for l in range(depth):
        row = lambda a: a[l].reshape(1, -1)
        zl = jnp.zeros((LORA, D_RWKV), F32)
        w_lora = jnp.concatenate(
            [jnp.concatenate([w_up[l], zl], axis=1), jnp.concatenate([zl, a_up[l]], axis=1)],
            axis=0).astype(BF16)
        in_params = (row(g_pre), w_in[l].astype(BF16), row(mu), row(w0), row(a0), w_lora,
                     row(k_k), row(k_a), row(r_k))
        post_params = (row(gn_r_g), row(gn_r_b), w_dw[l], row(b_dw), row(gn_c_g), row(gn_c_b),
                       w_out[l].astype(BF16), row(g_post))

        m = mod[l, :n_p]
        sh, sc, gt = (m[:, None, i * D_MODEL:(i + 1) * D_MODEL] for i in range(3))
        (r, lw, km, v, kk, kka, bonus, sgr, glu, sgc, hl) = _inproj_call(
            False, xp, sh, sc, zeros_shift, in_params, bones, PROMPT_TILE)
        y, s_new = _wkv_chunk_call(r, lw, km, v, kk, kka, zeros_wkv, lt)
        xp = _post_call(False, y, bonus, sgr, glu, sgc, xp, gt, zeros_conv, post_params, bones,
                        PROMPT_TILE)
        shifts_p.append(hl[:, 0])
        wkvs_p.append(s_new)
        convs_p.append(jnp.concatenate([zeros_conv, glu], axis=1)[:, -HIST:])

        m = mod[l, n_p:]
        sh, sc, gt = (m[None, :, i * D_MODEL:(i + 1) * D_MODEL] for i in range(3))
        (r, lw, km, v, kk, kka, bonus, sgr, glu, sgc, hl) = _inproj_call(
            True, xs, sh, sc, state_shift[l][None], in_params, bones, None)
        y, s_new = _wkv_seq_call(r, lw, km, v, kk, kka, state_wkv[l])
        xs = _post_call(True, y, bonus, sgr, glu, sgc, xs, gt,
                        jnp.transpose(state_conv[l], (1, 0, 2)), post_params, bones, None)
        shifts_s.append(hl[0])
        wkvs_s.append(s_new)
        convs_s.append(jnp.concatenate(
            [state_conv[l], jnp.transpose(glu, (1, 0, 2))], axis=1)[:, -HIST:])

    return (xp, jnp.transpose(xs, (1, 0, 2)),
            jnp.stack(shifts_p), jnp.stack(wkvs_p), jnp.stack(convs_p),
            jnp.stack(shifts_s), jnp.stack(wkvs_s), jnp.stack(convs_s))
```

```python
import functools

import jax
import jax.numpy as jnp
from jax import lax
from jax.experimental import pallas as pl
from jax.experimental.pallas import tpu as pltpu

F32 = jnp.float32
BF16 = jnp.bfloat16

D_MODEL = 1024
D_RWKV = 512
D_CONV = 512
HEAD_DIM = 64
N_HEADS = 8
N_PAIRS = N_HEADS // 2
LORA = 64
P_RWKV = 4 * D_RWKV + 2 * LORA
P_IN = P_RWKV + 3 * D_CONV
CONV_W = 31
HIST = CONV_W - 1
RMS_EPS = 1e-6
RWKV_GN_EPS = 64e-5
CONV_GN_EPS = 1e-5

LANES = 128
SUBLANES = 8
CHUNK = 64
VMEM_LIMIT = 56 * 1024 * 1024

_NN = (((1,), (0,)), ((), ()))
_NT = (((1,), (1,)), ((), ()))
_TN = (((0,), (0,)), ((), ()))


def _dg(a, b, dims=_NN):
    return lax.dot_general(a, b, dims, preferred_element_type=F32)


def _split2(x):
    hi = x.astype(BF16)
    lo = (x - hi.astype(F32)).astype(BF16)
    return hi, lo


def _mm(a, b, dims=_NN, passes=1):
    if passes == 1:
        return _dg(a.astype(BF16), b.astype(BF16), dims)
    ah, al = _split2(a)
    bh, bl = _split2(b)
    return _dg(ah, bh, dims) + (_dg(ah, bl, dims) + _dg(al, bh, dims))


def _segsum(x, bones):
    hi, lo = _split2(x)
    return _dg(hi, bones) + _dg(lo, bones)


def _sigmoid(x):
    return 1.0 / (1.0 + jnp.exp(-x))


def _group_norm(x, g, b, eps, bones):
    inv_n = 1.0 / HEAD_DIM
    mean = _segsum(x, bones) * inv_n
    d = x - mean
    var = _segsum(d * d, bones) * inv_n
    return d * lax.rsqrt(var + eps) * g + b


def _mod_body(c_ref, w_ref, b_ref, o_ref):
    c = c_ref[...]
    s = (c * _sigmoid(c)).astype(BF16)
    o_ref[0] = _dg(s, w_ref[0].astype(BF16)) + b_ref[0]


def _mod_call(c_all, w_ada, b_ada):
    depth, _, n3 = w_ada.shape
    rows = c_all.shape[0]
    bn = 768
    return pl.pallas_call(
        _mod_body,
        grid=(depth, n3 // bn),
        in_specs=[
            pl.BlockSpec((rows, D_MODEL), lambda l, j: (0, 0)),
            pl.BlockSpec((1, D_MODEL, bn), lambda l, j: (l, 0, j)),
            pl.BlockSpec((1, 1, bn), lambda l, j: (l, 0, j)),
        ],
        out_specs=pl.BlockSpec((1, rows, bn), lambda l, j: (l, 0, j)),
        out_shape=jax.ShapeDtypeStruct((depth, rows, n3), F32),
        compiler_params=pltpu.CompilerParams(
            dimension_semantics=("arbitrary", "arbitrary"), vmem_limit_bytes=VMEM_LIMIT),
        name="adaln_mod",
    )(c_all, w_ada, b_ada.reshape(depth, 1, n3))


def _inproj_body(time_major, x_ref, sh_ref, sc_ref, ss_ref, gpre_ref, win_ref, mu_ref,
                 w0_ref, a0_ref, wl_ref, kkp_ref, kap_ref, rkp_ref, bones_ref,
                 r_o, lw_o, km_o, v_o, kk_o, kka_o, bon_o, sgr_o, glu_o, sgc_o, hl_o, carry):
    step = pl.program_id(1)
    x = x_ref[0]
    rows = x.shape[0]
    ms = jnp.mean(x * x, axis=-1, keepdims=True)
    h = x * lax.rsqrt(ms + RMS_EPS) * gpre_ref[...]
    h = h * (1.0 + sc_ref[0]) + sh_ref[0]
    u = _dg(h.astype(BF16), win_ref[...])
    ur = u[:, :P_RWKV]

    if time_major:
        @pl.when(step == 0)
        def _():
            carry[...] = _dg(ss_ref[0].astype(BF16), win_ref[:, :P_RWKV])
        prev = carry[...]
        carry[...] = ur
        hl_o[0] = h
    else:
        @pl.when(step == 0)
        def _():
            ss8 = jnp.broadcast_to(ss_ref[0], (8, D_MODEL)).astype(BF16)
            carry[7:8, :] = _dg(ss8, win_ref[:, :P_RWKV])[0:1]
        carry[8:8 + rows, :] = ur
        prev = carry[7:7 + rows, :]
        carry[7:8, :] = ur[rows - 1:rows]
        hl_o[0] = h[rows - 1:rows]

    urs = ur + (prev - ur) * mu_ref[...]
    r = urs[:, 0:D_RWKV]
    k = urs[:, D_RWKV:2 * D_RWKV]
    v = urs[:, 2 * D_RWKV:3 * D_RWKV]
    g_r = urs[:, 3 * D_RWKV:4 * D_RWKV]
    lora = urs[:, 4 * D_RWKV:P_RWKV]
    lane = lax.broadcasted_iota(jnp.int32, lora.shape, 1)
    lora_in = jnp.where(lane < LORA, jnp.tanh(lora), lora).astype(BF16)
    wa = _dg(lora_in, wl_ref[...])
    z = -(w0_ref[...] + wa[:, :D_RWKV])
    softplus = jnp.maximum(z, 0.0) + jnp.log(1.0 + jnp.exp(-jnp.abs(z)))
    lw = -jnp.exp(-softplus - 0.5)
    a = _sigmoid(a0_ref[...] + wa[:, D_RWKV:])

    bones = bones_ref[...]
    kkr = k * kkp_ref[...]
    ssq = _segsum(kkr * kkr, bones)
    kk = kkr * lax.rsqrt(jnp.maximum(ssq, 1e-24))
    km = k * (1.0 + (a - 1.0) * kap_ref[...])
    bonus = _segsum(r * km * rkp_ref[...], bones) * v

    r_o[0] = r
    lw_o[0] = lw
    km_o[0] = km
    v_o[0] = v
    kk_o[0] = kk
    kka_o[0] = kk * a
    bon_o[0] = bonus
    sgr_o[0] = g_r * _sigmoid(g_r)

    glu_a = u[:, P_RWKV:P_RWKV + D_CONV]
    glu_b = u[:, P_RWKV + D_CONV:P_RWKV + 2 * D_CONV]
    g_c = u[:, P_RWKV + 2 * D_CONV:]
    glu_o[0] = glu_a * _sigmoid(glu_b)
    sgc_o[0] = g_c * _sigmoid(g_c)


def _inproj_call(time_major, x, shift, scale, s_shift, lw_params, bones, tile):
    n_outer, n_inner, _ = x.shape
    if time_major:
        grid = (1, n_outer)
        rows = n_inner
        x_map = lambda i, t: (t, 0, 0)
        row_map = lambda i, t: (0, 0, 0)
        row_block = (1, rows, D_MODEL)
        carry = pltpu.VMEM((rows, P_RWKV), F32)
        hl_shape = (1, rows, D_MODEL)
    else:
        rows = tile
        grid = (n_outer, n_inner // rows)
        x_map = lambda b, t: (b, t, 0)
        row_map = lambda b, t: (b, 0, 0)
        row_block = (1, 1, D_MODEL)
        carry = pltpu.VMEM((rows + 8, P_RWKV), F32)
        hl_shape = (n_outer, 1, D_MODEL)
    const = lambda i, t: (0, 0)
    g_pre, w_in, mu, w0, a0, w_lora, k_k, k_a, r_k = lw_params
    params = [g_pre, w_in, mu, w0, a0, w_lora, k_k, k_a, r_k, bones]
    out_block = pl.BlockSpec((1, rows, D_RWKV), x_map)
    out_sds = jax.ShapeDtypeStruct((n_outer, n_inner, D_RWKV), F32)
    return pl.pallas_call(
        functools.partial(_inproj_body, time_major),
        grid=grid,
        in_specs=[pl.BlockSpec((1, rows, D_MODEL), x_map)]
        + [pl.BlockSpec(row_block, row_map)] * 3
        + [pl.BlockSpec(p.shape, const) for p in params],
        out_specs=[out_block] * 10 + [pl.BlockSpec((1,) + hl_shape[1:], row_map)],
        out_shape=[out_sds] * 10 + [jax.ShapeDtypeStruct(hl_shape, F32)],
        scratch_shapes=[carry],
        compiler_params=pltpu.CompilerParams(
            dimension_semantics=("arbitrary", "arbitrary"), vmem_limit_bytes=VMEM_LIMIT),
        name="inproj_tm" if time_major else "inproj_bm",
    )(x, shift, scale, s_shift, *params)


PASS_GRAM = 1
PASS_INV = 1
PASS_STATE = 1
PASS_OUT = 1


def _wkv_chunk_body(r_ref, lw_ref, km_ref, v_ref, kk_ref, kka_ref, s0_ref, lt_ref,
                    y_ref, so_ref, s_scr):
    c = pl.program_id(1)
    zeros_hh = jnp.zeros((HEAD_DIM, HEAD_DIM), F32)

    @pl.when(c == 0)
    def _():
        for p in range(N_PAIRS):
            top = jnp.concatenate([s0_ref[0, 2 * p], zeros_hh], axis=1)
            bot = jnp.concatenate([zeros_hh, s0_ref[0, 2 * p + 1]], axis=1)
            s_scr[p] = jnp.concatenate([top, bot], axis=0)

    lw = lw_ref[0]
    l_hi = lw.astype(BF16)
    l_r1 = lw - l_hi.astype(F32)
    l_mid = l_r1.astype(BF16)
    l_lo = (l_r1 - l_mid.astype(F32)).astype(BF16)
    lt = lt_ref[...]
    cum = _dg(lt, l_hi) + (_dg(lt, l_mid) + _dg(lt, l_lo))
    cum_last = cum[CHUNK - 1:CHUNK, :]
    p_inc = jnp.exp(cum)
    p_exc = jnp.exp(cum - lw)
    p_inv = jnp.exp(-cum)
    p_end = jnp.exp(cum_last - cum)
    p_last = jnp.exp(cum_last)

    at_all = -kk_ref[0] * p_exc
    rt_all = r_ref[0] * p_inc
    kka = kka_ref[0]
    km = km_ref[0]
    bt_all = kka * p_inv
    kt_all = km * p_inv
    bh_all = kka * p_end
    kh_all = km * p_end
    v_all = v_ref[0]

    lane = lax.broadcasted_iota(jnp.int32, (CHUNK, LANES), 1)
    first = lane < HEAD_DIM
    row = lax.broadcasted_iota(jnp.int32, (LANES, LANES), 0)
    col = lax.broadcasted_iota(jnp.int32, (LANES, LANES), 1)
    same_head = (row < HEAD_DIM) == (col < HEAD_DIM)
    rt_i = row & (HEAD_DIM - 1)
    ct_i = col & (HEAD_DIM - 1)
    tri_strict = same_head & (ct_i < rt_i)
    tri_incl = same_head & (ct_i <= rt_i)
    eye = (row == col).astype(F32)

    def stack(xp):
        return jnp.concatenate([jnp.where(first, xp, 0.0), jnp.where(first, 0.0, xp)], axis=0)

    pairs = range(N_PAIRS)
    sls = [slice(LANES * p, LANES * (p + 1)) for p in pairs]
    x2 = [jnp.concatenate([stack(at_all[:, s]), stack(rt_all[:, s])], axis=0) for s in sls]
    y2 = [jnp.concatenate([stack(bt_all[:, s]), stack(kt_all[:, s])], axis=0) for s in sls]
    vs = [stack(v_all[:, s]) for s in sls]
    hk = [jnp.concatenate([stack(bh_all[:, s]), stack(kh_all[:, s])], axis=0) for s in sls]

    g = [_mm(x2[p], y2[p], _NT, PASS_GRAM) for p in pairs]
    n_pow = [jnp.where(tri_strict, g[p][:LANES, :LANES], 0.0) for p in pairs]
    a_ak = [jnp.where(tri_strict, g[p][:LANES, LANES:], 0.0) for p in pairs]
    a_r = [jnp.concatenate([jnp.where(tri_incl, g[p][LANES:, :LANES], 0.0),
                            jnp.where(tri_incl, g[p][LANES:, LANES:], 0.0)], axis=1) for p in pairs]

    t_inv = [eye + n_pow[p] for p in pairs]
    span = 2
    while span < CHUNK:
        n_pow = [_mm(n_pow[p], n_pow[p], _NN, PASS_INV) for p in pairs]
        t_inv = [t_inv[p] + _mm(t_inv[p], n_pow[p], _NN, PASS_INV) for p in pairs]
        span *= 2

    s_bd = [s_scr[p] for p in pairs]
    xs = [_mm(x2[p], s_bd[p], _NT, PASS_STATE) for p in pairs]
    akv = [_mm(a_ak[p], vs[p], _NN, PASS_STATE) for p in pairs]
    z_mat = [_mm(t_inv[p], xs[p][:LANES] + akv[p], _NN, PASS_STATE) for p in pairs]
    zv = [jnp.concatenate([z_mat[p], vs[p]], axis=0) for p in pairs]
    y_bd = [xs[p][LANES:] + _mm(a_r[p], zv[p], _NN, PASS_OUT) for p in pairs]
    s_new = [s_bd[p] * p_last[:, sls[p]] + _mm(zv[p], hk[p], _TN, PASS_STATE) for p in pairs]
    for p in pairs:
        y_ref[0, :, sls[p]] = y_bd[p][:HEAD_DIM] + y_bd[p][HEAD_DIM:]
        s_scr[p] = s_new[p]
        so_ref[0, 2 * p] = s_new[p][:HEAD_DIM, :HEAD_DIM]
        so_ref[0, 2 * p + 1] = s_new[p][HEAD_DIM:, HEAD_DIM:]


def _wkv_chunk_call(r, lw, km, v, kk, kka, s0, lt):
    n_b, n_t, _ = r.shape
    seq_block = pl.BlockSpec((1, CHUNK, D_RWKV), lambda b, c: (b, c, 0))
    st_block = pl.BlockSpec((1, N_HEADS, HEAD_DIM, HEAD_DIM), lambda b, c: (b, 0, 0, 0))
    return pl.pallas_call(
        _wkv_chunk_body,
        grid=(n_b, n_t // CHUNK),
        in_specs=[seq_block] * 6 + [st_block, pl.BlockSpec(lt.shape, lambda b, c: (0, 0))],
        out_specs=[seq_block, st_block],
        out_shape=[jax.ShapeDtypeStruct(r.shape, F32), jax.ShapeDtypeStruct(s0.shape, F32)],
        scratch_shapes=[pltpu.VMEM((N_PAIRS, LANES, LANES), F32)],
        compiler_params=pltpu.CompilerParams(
            dimension_semantics=("arbitrary", "arbitrary"), vmem_limit_bytes=VMEM_LIMIT),
        name="wkv_chunk",
    )(r, lw, km, v, kk, kka, s0, lt)


SEQ_BATCH_BLOCK = 8


def _wkv_seq_body(r_ref, lw_ref, km_ref, v_ref, kk_ref, kka_ref, s_ref, y_ref, so_ref):
    n_t = r_ref.shape[0]
    lane = lax.broadcasted_iota(jnp.int32, (HEAD_DIM, LANES), 1)
    rowi = lax.broadcasted_iota(jnp.int32, (HEAD_DIM, LANES), 0)
    first = lane < HEAD_DIM
    eye2 = ((lane & (HEAD_DIM - 1)) == rowi).astype(F32)

    def headsum(xt):
        s0 = jnp.sum(jnp.where(first, xt, 0.0), axis=-1, keepdims=True)
        s1 = jnp.sum(jnp.where(first, 0.0, xt), axis=-1, keepdims=True)
        return jnp.where(first, s0, s1)

    for b in range(s_ref.shape[0]):
        for p in range(N_PAIRS):
            s = jnp.concatenate([s_ref[b, 2 * p], s_ref[b, 2 * p + 1]], axis=1)
            for t in range(n_t):
                idx = (t, slice(b, b + 1), slice(LANES * p, LANES * (p + 1)))
                kk = kk_ref[idx]
                sk = headsum(s * kk)
                vcol = headsum(eye2 * v_ref[idx])
                s = s * jnp.exp(lw_ref[idx]) - sk * kka_ref[idx] + vcol * km_ref[idx]
                yb = headsum(s * r_ref[idx])
                y_ref[idx] = jnp.sum(yb * eye2, axis=0, keepdims=True)
            so_ref[b, 2 * p] = s[:, :HEAD_DIM]
            so_ref[b, 2 * p + 1] = s[:, HEAD_DIM:]


def _wkv_seq_call(r, lw, km, v, kk, kka, s0):
    n_t, n_b, _ = r.shape
    bb = SEQ_BATCH_BLOCK
    seq_block = pl.BlockSpec((n_t, bb, D_RWKV), lambda i: (0, i, 0))
    st_block = pl.BlockSpec((bb, N_HEADS, HEAD_DIM, HEAD_DIM), lambda i: (i, 0, 0, 0))
    return pl.pallas_call(
        _wkv_seq_body,
        grid=(n_b // bb,),
        in_specs=[seq_block] * 6 + [st_block],
        out_specs=[seq_block, st_block],
        out_shape=[jax.ShapeDtypeStruct(r.shape, F32), jax.ShapeDtypeStruct(s0.shape, F32)],
        compiler_params=pltpu.CompilerParams(
            dimension_semantics=("arbitrary",), vmem_limit_bytes=VMEM_LIMIT),
        name="wkv_seq",
    )(r, lw, km, v, kk, kka, s0)


CONV_ROW_BLOCK = 32
HIST_PAD = 32


def _merge_and_residual(y, bonus, sgr, conv, sgc, x, gate, gnr_g, gnr_b, gnc_g, gnc_b,
                        wout_ref, gpost, bones):
    y_r = (_group_norm(y, gnr_g, gnr_b, RWKV_GN_EPS, bones) + bonus) * sgr
    cn = _group_norm(conv, gnc_g, gnc_b, CONV_GN_EPS, bones)
    y_c = cn * _sigmoid(cn) * sgc
    mix = _dg(y_r.astype(BF16), wout_ref[0:D_RWKV, :]) + _dg(y_c.astype(BF16), wout_ref[D_RWKV:, :])
    ms = jnp.mean(mix * mix, axis=-1, keepdims=True)
    return x + gate * (mix * lax.rsqrt(ms + RMS_EPS) * gpost)


def _post_bm_body(y_ref, bon_ref, sgr_ref, glu_ref, sgc_ref, x_ref, gate_ref, sconv_ref,
                  gnrg_ref, gnrb_ref, wdw_ref, bdw_ref, gncg_ref, gncb_ref, wout_ref, gpost_ref,
                  bones_ref, xo_ref, buf, conv):
    step = pl.program_id(1)
    rows = y_ref.shape[1]
    off = HIST_PAD - HIST
    shifted_rows = rows + HIST_PAD - SUBLANES

    @pl.when(step == 0)
    def _():
        buf[0, 0:off, :] = jnp.zeros((off, D_CONV), F32)
        buf[0, off:HIST_PAD, :] = sconv_ref[0]

    buf[0, HIST_PAD:HIST_PAD + rows, :] = glu_ref[0]
    for q in range(1, SUBLANES):
        buf[q, 0:shifted_rows, :] = buf[0, q:q + shifted_rows, :]
    for r0 in range(0, rows, CONV_ROW_BLOCK):
        acc = jnp.broadcast_to(bdw_ref[...], (CONV_ROW_BLOCK, D_CONV))
        for j in range(CONV_W):
            q = (j + off) % SUBLANES
            base = r0 + j + off - q
            acc = acc + wdw_ref[j:j + 1, :] * buf[q, base:base + CONV_ROW_BLOCK, :]
        conv[r0:r0 + CONV_ROW_BLOCK, :] = acc
    buf[0, off:HIST_PAD, :] = buf[0, rows + off:rows + HIST_PAD, :]

    xo_ref[0] = _merge_and_residual(
        y_ref[0], bon_ref[0], sgr_ref[0], conv[...], sgc_ref[0], x_ref[0], gate_ref[0],
        gnrg_ref[...], gnrb_ref[...], gncg_ref[...], gncb_ref[...], wout_ref, gpost_ref[...],
        bones_ref[...])


def _post_tm_body(y_ref, bon_ref, sgr_ref, glu_ref, sgc_ref, x_ref, gate_ref, sconv_ref,
                  gnrg_ref, gnrb_ref, wdw_ref, bdw_ref, gncg_ref, gncb_ref, wout_ref, gpost_ref,
                  bones_ref, xo_ref):
    n_t, rows, _ = y_ref.shape
    for t in range(n_t):
        acc = jnp.broadcast_to(bdw_ref[...], (rows, D_CONV))
        for j in range(CONV_W):
            i = t + j
            src = sconv_ref[i] if i < HIST else glu_ref[i - HIST]
            acc = acc + wdw_ref[j:j + 1, :] * src
        xo_ref[t] = _merge_and_residual(
            y_ref[t], bon_ref[t], sgr_ref[t], acc, sgc_ref[t], x_ref[t], gate_ref[0],
            gnrg_ref[...], gnrb_ref[...], gncg_ref[...], gncb_ref[...], wout_ref, gpost_ref[...],
            bones_ref[...])


def _post_call(time_major, y, bonus, sgr, glu, sgc, x, gate, s_conv, post_params, bones, tile):
    n_outer, n_inner, _ = x.shape
    const = lambda i, t: (0, 0)
    if time_major:
        rows = n_inner
        grid = (1, 1)
        lead = n_outer
        x_map = lambda i, t: (0, 0, 0)
        gate_spec = pl.BlockSpec((1, rows, D_MODEL), lambda i, t: (0, 0, 0))
        sconv_spec = pl.BlockSpec(s_conv.shape, lambda i, t: (0, 0, 0))
        scratch = []
        body = _post_tm_body
    else:
        rows = tile
        grid = (n_outer, n_inner // rows)
        lead = 1
        x_map = lambda b, t: (b, t, 0)
        gate_spec = pl.BlockSpec((1, 1, D_MODEL), lambda b, t: (b, 0, 0))
        sconv_spec = pl.BlockSpec((1, HIST, D_CONV), lambda b, t: (b, 0, 0))
        scratch = [pltpu.VMEM((SUBLANES, HIST_PAD + rows, D_CONV), F32),
                   pltpu.VMEM((rows, D_CONV), F32)]
        body = _post_bm_body
    params = list(post_params) + [bones]
    half = pl.BlockSpec((lead, rows, D_RWKV), x_map)
    full = pl.BlockSpec((lead, rows, D_MODEL), x_map)
    return pl.pallas_call(
        body,
        grid=grid,
        in_specs=[half] * 5 + [full, gate_spec, sconv_spec]
        + [pl.BlockSpec(p.shape, const) for p in params],
        out_specs=full,
        out_shape=jax.ShapeDtypeStruct(x.shape, F32),
        scratch_shapes=scratch,
        compiler_params=pltpu.CompilerParams(
            dimension_semantics=("arbitrary", "arbitrary"), vmem_limit_bytes=VMEM_LIMIT),
        name="post_tm" if time_major else "post_bm",
    )(y, bonus, sgr, glu, sgc, x, gate, s_conv, *params)


PROMPT_TILE = 256


def _block_ones():
    i = lax.broadcasted_iota(jnp.int32, (D_RWKV, D_RWKV), 0) // HEAD_DIM
    j = lax.broadcasted_iota(jnp.int32, (D_RWKV, D_RWKV), 1) // HEAD_DIM
    return (i == j).astype(BF16)


def _lower_tri():
    i = lax.broadcasted_iota(jnp.int32, (CHUNK, CHUNK), 0)
    j = lax.broadcasted_iota(jnp.int32, (CHUNK, CHUNK), 1)
    return (j <= i).astype(BF16)


def kernel(x_prompt, x_sample, c_prompt, c_sample, state_shift, state_wkv, state_conv, w_ada, b_ada, g_pre, g_post, w_in, mu, w0, w_up, a0, a_up, k_k, k_a, r_k, gn_r_g, gn_r_b, w_dw, b_dw, gn_c_g, gn_c_b, w_out):
    depth = w_in.shape[0]
    n_p, t_p, _ = x_prompt.shape
    n_s, t_s, _ = x_sample.shape
    bones = _block_ones()
    lt = _lower_tri()

    mod = _mod_call(jnp.concatenate([c_prompt, c_sample], axis=0), w_ada, b_ada)

    xp = x_prompt
    xs = jnp.transpose(x_sample, (1, 0, 2))
    zeros_shift = jnp.zeros((n_p, 1, D_MODEL), F32)
    zeros_wkv = jnp.zeros((n_p, N_HEADS, HEAD_DIM, HEAD_DIM), F32)
    zeros_conv = jnp.zeros((n_p, HIST, D_CONV), F32)

    shifts_p, wkvs_p, convs_p, shifts_s, wkvs_s, convs_s = [], [], [], [], [], []
    for l in range(depth):
        row = lambda a: a[l].reshape(1, -1)
        zl = jnp.zeros((LORA, D_RWKV), F32)
        w_lora = jnp.concatenate(
            [jnp.concatenate([w_up[l], zl], axis=1), jnp.concatenate([zl, a_up[l]], axis=1)],
            axis=0).astype(BF16)
        in_params = (row(g_pre), w_in[l].astype(BF16), row(mu), row(w0), row(a0), w_lora,
                     row(k_k), row(k_a), row(r_k))
        post_params = (row(gn_r_g), row(gn_r_b), w_dw[l], row(b_dw), row(gn_c_g), row(gn_c_b),
                       w_out[l].astype(BF16), row(g_post))

        m = mod[l, :n_p]
        sh, sc, gt = (m[:, None, i * D_MODEL:(i + 1) * D_MODEL] for i in range(3))
        (r, lw, km, v, kk, kka, bonus, sgr, glu, sgc, hl) = _inproj_call(
            False, xp, sh, sc, zeros_shift, in_params, bones, PROMPT_TILE)
        y, s_new = _wkv_chunk_call(r, lw, km, v, kk, kka, zeros_wkv, lt)
        xp = _post_call(False, y, bonus, sgr, glu, sgc, xp, gt, zeros_conv, post_params, bones,
                        PROMPT_TILE)
        shifts_p.append(hl[:, 0])
        wkvs_p.append(s_new)
        convs_p.append(jnp.concatenate([zeros_conv, glu], axis=1)[:, -HIST:])

        m = mod[l, n_p:]
        sh, sc, gt = (m[None, :, i * D_MODEL:(i + 1) * D_MODEL] for i in range(3))
        (r, lw, km, v, kk, kka, bonus, sgr, glu, sgc, hl) = _inproj_call(
            True, xs, sh, sc, state_shift[l][None], in_params, bones, None)
        y, s_new = _wkv_seq_call(r, lw, km, v, kk, kka, state_wkv[l])
        xs = _post_call(True, y, bonus, sgr, glu, sgc, xs, gt,
                        jnp.transpose(state_conv[l], (1, 0, 2)), post_params, bones, None)
        shifts_s.append(hl[0])
        wkvs_s.append(s_new)
        convs_s.append(jnp.concatenate(
            [state_conv[l], jnp.transpose(glu, (1, 0, 2))], axis=1)[:, -HIST:])

    return (xp, jnp.transpose(xs, (1, 0, 2)),
            jnp.stack(shifts_p), jnp.stack(wkvs_p), jnp.stack(convs_p),
            jnp.stack(shifts_s), jnp.stack(wkvs_s), jnp.stack(convs_s))
```

```python
import functools

import jax
import jax.numpy as jnp
from jax import lax
from jax.experimental import pallas as pl
from jax.experimental.pallas import tpu as pltpu

F32 = jnp.float32
BF16 = jnp.bfloat16

D_MODEL = 1024
D_RWKV = 512
D_CONV = 512
HEAD_DIM = 64
N_HEADS = 8
N_PAIRS = N_HEADS // 2
LORA = 64
P_RWKV = 4 * D_RWKV + 2 * LORA
P_IN = P_RWKV + 3 * D_CONV
CONV_W = 31
HIST = CONV_W - 1
RMS_EPS = 1e-6
RWKV_GN_EPS = 64e-5
CONV_GN_EPS = 1e-5

LANES = 128
SUBLANES = 8
CHUNK = 64
VMEM_LIMIT = 56 * 1024 * 1024

_NN = (((1,), (0,)), ((), ()))
_NT = (((1,), (1,)), ((), ()))
_TN = (((0,), (0,)), ((), ()))


def _dg(a, b, dims=_NN):
    return lax.dot_general(a, b, dims, preferred_element_type=F32)


def _split2(x):
    hi = x.astype(BF16)
    lo = (x - hi.astype(F32)).astype(BF16)
    return hi, lo


def _mm(a, b, dims=_NN, passes=1):
    if passes == 1:
        return _dg(a.astype(BF16), b.astype(BF16), dims)
    ah, al = _split2(a)
    bh, bl = _split2(b)
    return _dg(ah, bh, dims) + (_dg(ah, bl, dims) + _dg(al, bh, dims))


def _segsum(x, bones):
    hi, lo = _split2(x)
    return _dg(hi, bones) + _dg(lo, bones)


def _sigmoid(x):
    return 1.0 / (1.0 + jnp.exp(-x))


def _group_norm(x, g, b, eps, bones):
    inv_n = 1.0 / HEAD_DIM
    mean = _segsum(x, bones) * inv_n
    d = x - mean
    var = _segsum(d * d, bones) * inv_n
    return d * lax.rsqrt(var + eps) * g + b


def _mod_body(c_ref, w_ref, b_ref, o_ref):
    c = c_ref[...]
    s = (c * _sigmoid(c)).astype(BF16)
    o_ref[0] = _dg(s, w_ref[0].astype(BF16)) + b_ref[0]


def _mod_call(c_all, w_ada, b_ada):
    depth, _, n3 = w_ada.shape
    rows = c_all.shape[0]
    bn = 768
    return pl.pallas_call(
        _mod_body,
        grid=(depth, n3 // bn),
        in_specs=[
            pl.BlockSpec((rows, D_MODEL), lambda l, j: (0, 0)),
            pl.BlockSpec((1, D_MODEL, bn), lambda l, j: (l, 0, j)),
            pl.BlockSpec((1, 1, bn), lambda l, j: (l, 0, j)),
        ],
        out_specs=pl.BlockSpec((1, rows, bn), lambda l, j: (l, 0, j)),
        out_shape=jax.ShapeDtypeStruct((depth, rows, n3), F32),
        compiler_params=pltpu.CompilerParams(
            dimension_semantics=("arbitrary", "arbitrary"), vmem_limit_bytes=VMEM_LIMIT),
        name="adaln_mod",
    )(c_all, w_ada, b_ada.reshape(depth, 1, n3))


def _inproj_body(time_major, x_ref, sh_ref, sc_ref, ss_ref, gpre_ref, win_ref, mu_ref,
                 w0_ref, a0_ref, wl_ref, kkp_ref, kap_ref, rkp_ref, bones_ref,
                 r_o, lw_o, km_o, v_o, kk_o, kka_o, bon_o, sgr_o, glu_o, sgc_o, hl_o, carry):
    step = pl.program_id(1)
    x = x_ref[0]
    rows = x.shape[0]
    ms = jnp.mean(x * x, axis=-1, keepdims=True)
    h = x * lax.rsqrt(ms + RMS_EPS) * gpre_ref[...]
    h = h * (1.0 + sc_ref[0]) + sh_ref[0]
    hb = h.astype(BF16)
    ur = _dg(hb, win_ref[:, :P_RWKV])

    if time_major:
        @pl.when(step == 0)
        def _():
            carry[...] = _dg(ss_ref[0].astype(BF16), win_ref[:, :P_RWKV])
        prev = carry[...]
        carry[...] = ur
        hl_o[0] = h
    else:
        @pl.when(step == 0)
        def _():
            ss8 = jnp.broadcast_to(ss_ref[0], (SUBLANES, D_MODEL)).astype(BF16)
            carry[SUBLANES - 1:SUBLANES, :] = _dg(ss8, win_ref[:, :P_RWKV])[0:1]
        carry[SUBLANES:SUBLANES + rows, :] = ur
        prev = carry[SUBLANES - 1:SUBLANES - 1 + rows, :]
        carry[SUBLANES - 1:SUBLANES, :] = ur[rows - 1:rows]
        hl_o[0] = h[rows - 1:rows]

    def conv_cols(j):
        c0 = P_RWKV + j * D_CONV
        return _dg(hb, win_ref[:, c0:c0 + D_CONV])

    urs = ur + (prev - ur) * mu_ref[...]
    r = urs[:, 0:D_RWKV]
    k = urs[:, D_RWKV:2 * D_RWKV]
    v = urs[:, 2 * D_RWKV:3 * D_RWKV]
    g_r = urs[:, 3 * D_RWKV:4 * D_RWKV]
    lora = urs[:, 4 * D_RWKV:P_RWKV]
    glu_a = conv_cols(0)
    lane = lax.broadcasted_iota(jnp.int32, lora.shape, 1)
    lora_in = jnp.where(lane < LORA, jnp.tanh(lora), lora).astype(BF16)
    wa = _dg(lora_in, wl_ref[...])
    z = -(w0_ref[...] + wa[:, :D_RWKV])
    softplus = jnp.maximum(z, 0.0) + jnp.log(1.0 + jnp.exp(-jnp.abs(z)))
    lw = -jnp.exp(-softplus - 0.5)
    a = _sigmoid(a0_ref[...] + wa[:, D_RWKV:])
    lw_o[0] = lw
    r_o[0] = r
    v_o[0] = v

    glu_b = conv_cols(1)
    bones = bones_ref[...]
    kkr = k * kkp_ref[...]
    ssq = _segsum(kkr * kkr, bones)
    kk = kkr * lax.rsqrt(jnp.maximum(ssq, 1e-24))
    kk_o[0] = kk
    kka_o[0] = kk * a

    g_c = conv_cols(2)
    km = k * (1.0 + (a - 1.0) * kap_ref[...])
    km_o[0] = km
    bon_o[0] = _segsum(r * km * rkp_ref[...], bones) * v
    sgr_o[0] = g_r * _sigmoid(g_r)

    glu_o[0] = glu_a * _sigmoid(glu_b)
    sgc_o[0] = g_c * _sigmoid(g_c)


def _inproj_call(time_major, x, shift, scale, s_shift, lw_params, bones, tile):
    n_outer, n_inner, _ = x.shape
    if time_major:
        grid = (1, n_outer)
        rows = n_inner
        x_map = lambda i, t: (t, 0, 0)
        row_map = lambda i, t: (0, 0, 0)
        row_block = (1, rows, D_MODEL)
        carry = pltpu.VMEM((rows, P_RWKV), F32)
        hl_shape = (1, rows, D_MODEL)
    else:
        rows = tile
        grid = (n_outer, n_inner // rows)
        x_map = lambda b, t: (b, t, 0)
        row_map = lambda b, t: (b, 0, 0)
        row_block = (1, 1, D_MODEL)
        carry = pltpu.VMEM((rows + 8, P_RWKV), F32)
        hl_shape = (n_outer, 1, D_MODEL)
    const = lambda i, t: (0, 0)
    g_pre, w_in, mu, w0, a0, w_lora, k_k, k_a, r_k = lw_params
    params = [g_pre, w_in, mu, w0, a0, w_lora, k_k, k_a, r_k, bones]
    out_block = pl.BlockSpec((1, rows, D_RWKV), x_map)
    out_sds = jax.ShapeDtypeStruct((n_outer, n_inner, D_RWKV), F32)
    return pl.pallas_call(
        functools.partial(_inproj_body, time_major),
        grid=grid,
        in_specs=[pl.BlockSpec((1, rows, D_MODEL), x_map)]
        + [pl.BlockSpec(row_block, row_map)] * 3
        + [pl.BlockSpec(p.shape, const) for p in params],
        out_specs=[out_block] * 10 + [pl.BlockSpec((1,) + hl_shape[1:], row_map)],
        out_shape=[out_sds] * 10 + [jax.ShapeDtypeStruct(hl_shape, F32)],
        scratch_shapes=[carry],
        compiler_params=pltpu.CompilerParams(
            dimension_semantics=("arbitrary", "arbitrary"), vmem_limit_bytes=VMEM_LIMIT),
        name="inproj_tm" if time_major else "inproj_bm",
    )(x, shift, scale, s_shift, *params)


PASS_GRAM = 1
PASS_INV = 1
PASS_STATE = 1
PASS_OUT = 1
CHUNK_SEQS = 4


def _wkv_chunk_body(r_ref, lw_ref, km_ref, v_ref, kk_ref, kka_ref, s0_ref, lt_ref,
                    y_ref, so_ref, s_scr):
    c = pl.program_id(1)
    n_seq = r_ref.shape[0]
    zeros_hh = jnp.zeros((HEAD_DIM, HEAD_DIM), F32)

    @pl.when(c == 0)
    def _():
        for i in range(n_seq):
            for p in range(N_PAIRS):
                top = jnp.concatenate([s0_ref[i, 2 * p], zeros_hh], axis=1)
                bot = jnp.concatenate([zeros_hh, s0_ref[i, 2 * p + 1]], axis=1)
                s_scr[i * N_PAIRS + p] = jnp.concatenate([top, bot], axis=0)

    lt = lt_ref[...]
    lane = lax.broadcasted_iota(jnp.int32, (CHUNK, LANES), 1)
    first = lane < HEAD_DIM
    row = lax.broadcasted_iota(jnp.int32, (LANES, LANES), 0)
    col = lax.broadcasted_iota(jnp.int32, (LANES, LANES), 1)
    same_head = (row < HEAD_DIM) == (col < HEAD_DIM)
    rt_i = row & (HEAD_DIM - 1)
    ct_i = col & (HEAD_DIM - 1)
    tri_strict = same_head & (ct_i < rt_i)
    tri_incl = same_head & (ct_i <= rt_i)
    eye = (row == col).astype(F32)

    def stack(xp):
        return jnp.concatenate([jnp.where(first, xp, 0.0), jnp.where(first, 0.0, xp)], axis=0)

    x2, y2, vs, hk, p_last = [], [], [], [], []
    for i in range(n_seq):
        lw = lw_ref[i]
        l_hi = lw.astype(BF16)
        l_r1 = lw - l_hi.astype(F32)
        l_mid = l_r1.astype(BF16)
        l_lo = (l_r1 - l_mid.astype(F32)).astype(BF16)
        cum = _dg(lt, l_hi) + (_dg(lt, l_mid) + _dg(lt, l_lo))
        cum_last = cum[CHUNK - 1:CHUNK, :]
        p_inv = jnp.exp(-cum)
        p_end = jnp.exp(cum_last - cum)
        at_all = -kk_ref[i] * jnp.exp(cum - lw)
        rt_all = r_ref[i] * jnp.exp(cum)
        kka = kka_ref[i]
        km = km_ref[i]
        bt_all = kka * p_inv
        kt_all = km * p_inv
        bh_all = kka * p_end
        kh_all = km * p_end
        v_all = v_ref[i]
        pl_all = jnp.exp(cum_last)
        for p in range(N_PAIRS):
            s = slice(LANES * p, LANES * (p + 1))
            x2.append(jnp.concatenate([stack(at_all[:, s]), stack(rt_all[:, s])], axis=0))
            y2.append(jnp.concatenate([stack(bt_all[:, s]), stack(kt_all[:, s])], axis=0))
            vs.append(stack(v_all[:, s]))
            hk.append(jnp.concatenate([stack(bh_all[:, s]), stack(kh_all[:, s])], axis=0))
            p_last.append(pl_all[:, s])

    units = range(n_seq * N_PAIRS)
    g = [_mm(x2[u], y2[u], _NT, PASS_GRAM) for u in units]
    n_pow = [jnp.where(tri_strict, g[u][:LANES, :LANES], 0.0) for u in units]
    a_ak = [jnp.where(tri_strict, g[u][:LANES, LANES:], 0.0) for u in units]
    a_r = [jnp.concatenate([jnp.where(tri_incl, g[u][LANES:, :LANES], 0.0),
                            jnp.where(tri_incl, g[u][LANES:, LANES:], 0.0)], axis=1) for u in units]

    t_inv = [eye + n_pow[u] for u in units]
    span = 2
    while span < CHUNK:
        n_pow = [_mm(n_pow[u], n_pow[u], _NN, PASS_INV) for u in units]
        t_inv = [t_inv[u] + _mm(t_inv[u], n_pow[u], _NN, PASS_INV) for u in units]
        span *= 2

    s_bd = [s_scr[u] for u in units]
    xs = [_mm(x2[u], s_bd[u], _NT, PASS_STATE) for u in units]
    akv = [_mm(a_ak[u], vs[u], _NN, PASS_STATE) for u in units]
    z_mat = [_mm(t_inv[u], xs[u][:LANES] + akv[u], _NN, PASS_STATE) for u in units]
    zv = [jnp.concatenate([z_mat[u], vs[u]], axis=0) for u in units]
    y_bd = [xs[u][LANES:] + _mm(a_r[u], zv[u], _NN, PASS_OUT) for u in units]
    s_new = [s_bd[u] * p_last[u] + _mm(zv[u], hk[u], _TN, PASS_STATE) for u in units]
    for u in units:
        i, p = divmod(u, N_PAIRS)
        y_ref[i, :, LANES * p:LANES * (p + 1)] = y_bd[u][:HEAD_DIM] + y_bd[u][HEAD_DIM:]
        s_scr[u] = s_new[u]
        so_ref[i, 2 * p] = s_new[u][:HEAD_DIM, :HEAD_DIM]
        so_ref[i, 2 * p + 1] = s_new[u][HEAD_DIM:, HEAD_DIM:]


def _wkv_chunk_call(r, lw, km, v, kk, kka, s0, lt):
    n_b, n_t, _ = r.shape
    nq = CHUNK_SEQS
    seq_block = pl.BlockSpec((nq, CHUNK, D_RWKV), lambda b, c: (b, c, 0))
    st_block = pl.BlockSpec((nq, N_HEADS, HEAD_DIM, HEAD_DIM), lambda b, c: (b, 0, 0, 0))
    return pl.pallas_call(
        _wkv_chunk_body,
        grid=(n_b // nq, n_t // CHUNK),
        in_specs=[seq_block] * 6 + [st_block, pl.BlockSpec(lt.shape, lambda b, c: (0, 0))],
        out_specs=[seq_block, st_block],
        out_shape=[jax.ShapeDtypeStruct(r.shape, F32), jax.ShapeDtypeStruct(s0.shape, F32)],
        scratch_shapes=[pltpu.VMEM((nq * N_PAIRS, LANES, LANES), F32)],
        compiler_params=pltpu.CompilerParams(
            dimension_semantics=("arbitrary", "arbitrary"), vmem_limit_bytes=VMEM_LIMIT),
        name="wkv_chunk",
    )(r, lw, km, v, kk, kka, s0, lt)


SHORT_T = 4
SHORT_SEQS = LANES // (2 * SHORT_T)


def _wkv_short_body(r_ref, lw_ref, km_ref, v_ref, kk_ref, kka_ref, s_ref, y_ref, so_ref):
    n_seq = s_ref.shape[0]
    grp = 2 * SHORT_T
    row = lax.broadcasted_iota(jnp.int32, (LANES, LANES), 0)
    col = lax.broadcasted_iota(jnp.int32, (LANES, LANES), 1)
    same_blk = (row // SHORT_T) == (col // SHORT_T)
    t_row = row % SHORT_T
    t_col = col % SHORT_T
    strict = same_blk & (t_col < t_row)
    incl = same_blk & (t_col <= t_row)
    eye = (row == col).astype(F32)
    sum_mat = jnp.concatenate([incl.astype(BF16), same_blk.astype(BF16)], axis=0)
    own = ((row // SHORT_T) % 2) == (col // HEAD_DIM)
    zeros_hh = jnp.zeros((HEAD_DIM, HEAD_DIM), F32)

    def rows_of(ref, p):
        return ref[:, :, LANES * p:LANES * (p + 1)].reshape(n_seq * grp, LANES)

    def mask(x):
        return jnp.where(own, x, 0.0)

    pairs = range(N_PAIRS)
    x2, y2, vs, bh, kh, p_last = [], [], [], [], [], []
    for p in pairs:
        lw = rows_of(lw_ref, p)
        l_hi = lw.astype(BF16)
        l_r1 = lw - l_hi.astype(F32)
        l_mid = l_r1.astype(BF16)
        l_lo = (l_r1 - l_mid.astype(F32)).astype(BF16)
        sums = _dg(sum_mat, l_hi) + (_dg(sum_mat, l_mid) + _dg(sum_mat, l_lo))
        cum = sums[:LANES]
        tot = sums[LANES:]
        p_inv = jnp.exp(-cum)
        p_end = jnp.exp(tot - cum)
        kka = rows_of(kka_ref, p)
        km = rows_of(km_ref, p)
        x2.append(jnp.concatenate([mask(-rows_of(kk_ref, p) * jnp.exp(cum - lw)),
                                   mask(rows_of(r_ref, p) * jnp.exp(cum))], axis=0))
        y2.append(jnp.concatenate([mask(kka * p_inv), mask(km * p_inv)], axis=0))
        vs.append(mask(rows_of(v_ref, p)))
        bh.append(mask(kka * p_end))
        kh.append(mask(km * p_end))
        p_last.append(jnp.exp(tot))

    g = [_mm(x2[p], y2[p], _NT, PASS_GRAM) for p in pairs]
    n_mat = [jnp.where(strict, g[p][:LANES, :LANES], 0.0) for p in pairs]
    a_ak = [jnp.where(strict, g[p][:LANES, LANES:], 0.0) for p in pairs]
    a_r = [jnp.concatenate([jnp.where(incl, g[p][LANES:, :LANES], 0.0),
                            jnp.where(incl, g[p][LANES:, LANES:], 0.0)], axis=1) for p in pairs]
    n_sq = [_mm(n_mat[p], n_mat[p], _NN, PASS_INV) for p in pairs]
    t_inv = [(eye + n_mat[p]) + _mm(eye + n_mat[p], n_sq[p], _NN, PASS_INV) for p in pairs]
    akv = [_mm(a_ak[p], vs[p], _NN, PASS_STATE) for p in pairs]

    def seq_rows(x, i):
        return x[grp * i:grp * (i + 1)]

    s_bd, xa_s, xr_s = [], [], []
    for p in pairs:
        s_p, xa_p, xr_p = [], [], []
        for i in range(n_seq):
            top = jnp.concatenate([s_ref[i, 2 * p], zeros_hh], axis=1)
            bot = jnp.concatenate([zeros_hh, s_ref[i, 2 * p + 1]], axis=1)
            s_i = jnp.concatenate([top, bot], axis=0)
            x_i = jnp.concatenate([seq_rows(x2[p][:LANES], i), seq_rows(x2[p][LANES:], i)], axis=0)
            xs_i = _mm(x_i, s_i, _NT, PASS_STATE)
            s_p.append(s_i)
            xa_p.append(xs_i[:grp])
            xr_p.append(xs_i[grp:])
        s_bd.append(s_p)
        xa_s.append(jnp.concatenate(xa_p, axis=0))
        xr_s.append(jnp.concatenate(xr_p, axis=0))

    z_mat = [_mm(t_inv[p], xa_s[p] + akv[p], _NN, PASS_STATE) for p in pairs]
    zv = [jnp.concatenate([z_mat[p], vs[p]], axis=0) for p in pairs]
    y_bd = [xr_s[p] + _mm(a_r[p], zv[p], _NN, PASS_OUT) for p in pairs]
    for p in pairs:
        y3 = y_bd[p].reshape(n_seq, grp, LANES)
        y_ref[:, :, LANES * p:LANES * (p + 1)] = y3 + jnp.concatenate(
            [y3[:, SHORT_T:], y3[:, :SHORT_T]], axis=1)
        for i in range(n_seq):
            zv_i = jnp.concatenate([seq_rows(z_mat[p], i), seq_rows(vs[p], i)], axis=0)
            hk_i = jnp.concatenate([seq_rows(bh[p], i), seq_rows(kh[p], i)], axis=0)
            s_new = (s_bd[p][i] * p_last[p][grp * i:grp * i + 1]
                     + _mm(zv_i, hk_i, _TN, PASS_STATE))
            so_ref[i, 2 * p] = s_new[:HEAD_DIM, :HEAD_DIM]
            so_ref[i, 2 * p + 1] = s_new[HEAD_DIM:, HEAD_DIM:]


def _wkv_short_call(r, lw, km, v, kk, kka, s0):
    n_b = r.shape[0]
    nq = SHORT_SEQS
    seq_block = pl.BlockSpec((nq, 2 * SHORT_T, D_RWKV), lambda i: (i, 0, 0))
    st_block = pl.BlockSpec((nq, N_HEADS, HEAD_DIM, HEAD_DIM), lambda i: (i, 0, 0, 0))
    return pl.pallas_call(
        _wkv_short_body,
        grid=(n_b // nq,),
        in_specs=[seq_block] * 6 + [st_block],
        out_specs=[seq_block, st_block],
        out_shape=[jax.ShapeDtypeStruct(r.shape, F32), jax.ShapeDtypeStruct(s0.shape, F32)],
        compiler_params=pltpu.CompilerParams(
            dimension_semantics=("arbitrary",), vmem_limit_bytes=VMEM_LIMIT),
        name="wkv_short",
    )(r, lw, km, v, kk, kka, s0)


CONV_ROW_BLOCK = 32
HIST_PAD = 32


def _merge_and_residual(y, bonus, sgr, conv, sgc, x, gate, gnr_g, gnr_b, gnc_g, gnc_b,
                        wout_ref, gpost, bones):
    y_r = (_group_norm(y, gnr_g, gnr_b, RWKV_GN_EPS, bones) + bonus) * sgr
    cn = _group_norm(conv, gnc_g, gnc_b, CONV_GN_EPS, bones)
    y_c = cn * _sigmoid(cn) * sgc
    mix = _dg(y_r.astype(BF16), wout_ref[0:D_RWKV, :]) + _dg(y_c.astype(BF16), wout_ref[D_RWKV:, :])
    ms = jnp.mean(mix * mix, axis=-1, keepdims=True)
    return x + gate * (mix * lax.rsqrt(ms + RMS_EPS) * gpost)


def _post_bm_body(y_ref, bon_ref, sgr_ref, glu_ref, sgc_ref, x_ref, gate_ref, sconv_ref,
                  gnrg_ref, gnrb_ref, wdw_ref, bdw_ref, gncg_ref, gncb_ref, wout_ref, gpost_ref,
                  bones_ref, xo_ref, buf, conv):
    step = pl.program_id(1)
    rows = y_ref.shape[1]
    off = HIST_PAD - HIST
    shifted_rows = rows + HIST_PAD - SUBLANES

    @pl.when(step == 0)
    def _():
        buf[0, 0:off, :] = jnp.zeros((off, D_CONV), F32)
        buf[0, off:HIST_PAD, :] = sconv_ref[0]

    buf[0, HIST_PAD:HIST_PAD + rows, :] = glu_ref[0]
    for q in range(1, SUBLANES):
        buf[q, 0:shifted_rows, :] = buf[0, q:q + shifted_rows, :]
    for r0 in range(0, rows, CONV_ROW_BLOCK):
        acc = jnp.broadcast_to(bdw_ref[...], (CONV_ROW_BLOCK, D_CONV))
        for j in range(CONV_W):
            q = (j + off) % SUBLANES
            base = r0 + j + off - q
            acc = acc + wdw_ref[j:j + 1, :] * buf[q, base:base + CONV_ROW_BLOCK, :]
        conv[r0:r0 + CONV_ROW_BLOCK, :] = acc
    buf[0, off:HIST_PAD, :] = buf[0, rows + off:rows + HIST_PAD, :]

    xo_ref[0] = _merge_and_residual(
        y_ref[0], bon_ref[0], sgr_ref[0], conv[...], sgc_ref[0], x_ref[0], gate_ref[0],
        gnrg_ref[...], gnrb_ref[...], gncg_ref[...], gncb_ref[...], wout_ref, gpost_ref[...],
        bones_ref[...])


def _post_tm_body(y_ref, bon_ref, sgr_ref, glu_ref, sgc_ref, x_ref, gate_ref, sconv_ref,
                  gnrg_ref, gnrb_ref, wdw_ref, bdw_ref, gncg_ref, gncb_ref, wout_ref, gpost_ref,
                  bones_ref, xo_ref):
    n_t, rows, _ = y_ref.shape
    for t in range(n_t):
        acc = jnp.broadcast_to(bdw_ref[...], (rows, D_CONV))
        for j in range(CONV_W):
            i = t + j
            src = sconv_ref[i] if i < HIST else glu_ref[i - HIST]
            acc = acc + wdw_ref[j:j + 1, :] * src
        xo_ref[t] = _merge_and_residual(
            y_ref[t], bon_ref[t], sgr_ref[t], acc, sgc_ref[t], x_ref[t], gate_ref[0],
            gnrg_ref[...], gnrb_ref[...], gncg_ref[...], gncb_ref[...], wout_ref, gpost_ref[...],
            bones_ref[...])


def _post_call(time_major, y, bonus, sgr, glu, sgc, x, gate, s_conv, post_params, bones, tile):
    n_outer, n_inner, _ = x.shape
    const = lambda i, t: (0, 0)
    if time_major:
        rows = n_inner
        grid = (1, 1)
        lead = n_outer
        x_map = lambda i, t: (0, 0, 0)
        gate_spec = pl.BlockSpec((1, rows, D_MODEL), lambda i, t: (0, 0, 0))
        sconv_spec = pl.BlockSpec(s_conv.shape, lambda i, t: (0, 0, 0))
        scratch = []
        body = _post_tm_body
    else:
        rows = tile
        grid = (n_outer, n_inner // rows)
        lead = 1
        x_map = lambda b, t: (b, t, 0)
        gate_spec = pl.BlockSpec((1, 1, D_MODEL), lambda b, t: (b, 0, 0))
        sconv_spec = pl.BlockSpec((1, HIST, D_CONV), lambda b, t: (b, 0, 0))
        scratch = [pltpu.VMEM((SUBLANES, HIST_PAD + rows, D_CONV), F32),
                   pltpu.VMEM((rows, D_CONV), F32)]
        body = _post_bm_body
    params = list(post_params) + [bones]
    half = pl.BlockSpec((lead, rows, D_RWKV), x_map)
    full = pl.BlockSpec((lead, rows, D_MODEL), x_map)
    return pl.pallas_call(
        body,
        grid=grid,
        in_specs=[half] * 5 + [full, gate_spec, sconv_spec]
        + [pl.BlockSpec(p.shape, const) for p in params],
        out_specs=full,
        out_shape=jax.ShapeDtypeStruct(x.shape, F32),
        scratch_shapes=scratch,
        compiler_params=pltpu.CompilerParams(
            dimension_semantics=("arbitrary", "arbitrary"), vmem_limit_bytes=VMEM_LIMIT),
        name="post_tm" if time_major else "post_bm",
    )(y, bonus, sgr, glu, sgc, x, gate, s_conv, *params)


PROMPT_TILE = 256


def _block_ones():
    i = lax.broadcasted_iota(jnp.int32, (D_RWKV, D_RWKV), 0) // HEAD_DIM
    j = lax.broadcasted_iota(jnp.int32, (D_RWKV, D_RWKV), 1) // HEAD_DIM
    return (i == j).astype(BF16)


def _lower_tri():
    i = lax.broadcasted_iota(jnp.int32, (CHUNK, CHUNK), 0)
    j = lax.broadcasted_iota(jnp.int32, (CHUNK, CHUNK), 1)
    return (j <= i).astype(BF16)


def kernel(x_prompt, x_sample, c_prompt, c_sample, state_shift, state_wkv, state_conv, w_ada, b_ada, g_pre, g_post, w_in, mu, w0, w_up, a0, a_up, k_k, k_a, r_k, gn_r_g, gn_r_b, w_dw, b_dw, gn_c_g, gn_c_b, w_out):
    depth = w_in.shape[0]
    n_p, t_p, _ = x_prompt.shape
    n_s, t_s, _ = x_sample.shape
    bones = _block_ones()
    lt = _lower_tri()

    mod = _mod_call(jnp.concatenate([c_prompt, c_sample], axis=0), w_ada, b_ada)

    xp = x_prompt
    xs = jnp.transpose(x_sample, (1, 0, 2))
    zeros_shift = jnp.zeros((n_p, 1, D_MODEL), F32)
    zeros_wkv = jnp.zeros((n_p, N_HEADS, HEAD_DIM, HEAD_DIM), F32)
    zeros_conv = jnp.zeros((n_p, HIST, D_CONV), F32)

    def dup_rows(a):
        a = jnp.transpose(a, (1, 0, 2))
        return jnp.concatenate([a, a], axis=1)

    shifts_p, wkvs_p, convs_p, shifts_s, wkvs_s, convs_s = [], [], [], [], [], []
    for l in range(depth):
        row = lambda a: a[l].reshape(1, -1)
        zl = jnp.zeros((LORA, D_RWKV), F32)
        w_lora = jnp.concatenate(
            [jnp.concatenate([w_up[l], zl], axis=1), jnp.concatenate([zl, a_up[l]], axis=1)],
            axis=0).astype(BF16)
        in_params = (row(g_pre), w_in[l].astype(BF16), row(mu), row(w0), row(a0), w_lora,
                     row(k_k), row(k_a), row(r_k))
        post_params = (row(gn_r_g), row(gn_r_b), w_dw[l], row(b_dw), row(gn_c_g), row(gn_c_b),
                       w_out[l].astype(BF16), row(g_post))

        m = mod[l, :n_p]
        sh, sc, gt = (m[:, None, i * D_MODEL:(i + 1) * D_MODEL] for i in range(3))
        (r, lw, km, v, kk, kka, bonus, sgr, glu, sgc, hl) = _inproj_call(
            False, xp, sh, sc, zeros_shift, in_params, bones, PROMPT_TILE)
        y, s_new = _wkv_chunk_call(r, lw, km, v, kk, kka, zeros_wkv, lt)
        xp = _post_call(False, y, bonus, sgr, glu, sgc, xp, gt, zeros_conv, post_params, bones,
                        PROMPT_TILE)
        shifts_p.append(hl[:, 0])
        wkvs_p.append(s_new)
        convs_p.append(jnp.concatenate([zeros_conv, glu], axis=1)[:, -HIST:])

        m = mod[l, n_p:]
        sh, sc, gt = (m[None, :, i * D_MODEL:(i + 1) * D_MODEL] for i in range(3))
        (r, lw, km, v, kk, kka, bonus, sgr, glu, sgc, hl) = _inproj_call(
            True, xs, sh, sc, state_shift[l][None], in_params, bones, None)
        y, s_new = _wkv_short_call(*(dup_rows(a) for a in (r, lw, km, v, kk, kka)), state_wkv[l])
        y = jnp.transpose(y[:, :SHORT_T], (1, 0, 2))
        xs = _post_call(True, y, bonus, sgr, glu, sgc, xs, gt,
                        jnp.transpose(state_conv[l], (1, 0, 2)), post_params, bones, None)
        shifts_s.append(hl[0])
        wkvs_s.append(s_new)
        convs_s.append(jnp.concatenate(
            [state_conv[l], jnp.transpose(glu, (1, 0, 2))], axis=1)[:, -HIST:])

    return (xp, jnp.transpose(xs, (1, 0, 2)),
            jnp.stack(shifts_p), jnp.stack(wkvs_p), jnp.stack(convs_p),
            jnp.stack(shifts_s), jnp.stack(wkvs_s), jnp.stack(convs_s))
```

```python
import functools

import jax
import jax.numpy as jnp
from jax import lax
from jax.experimental import pallas as pl
from jax.experimental.pallas import tpu as pltpu

F32 = jnp.float32
BF16 = jnp.bfloat16

D_MODEL = 1024
D_RWKV = 512
D_CONV = 512
HEAD_DIM = 64
N_HEADS = 8
N_PAIRS = N_HEADS // 2
LORA = 64
P_RWKV = 4 * D_RWKV + 2 * LORA
P_IN = P_RWKV + 3 * D_CONV
CONV_W = 31
HIST = CONV_W - 1
RMS_EPS = 1e-6
RWKV_GN_EPS = 64e-5
CONV_GN_EPS = 1e-5

LANES = 128
SUBLANES = 8
MXU_DIM = 256
CHUNK = 64
VMEM_LIMIT = 56 * 1024 * 1024

_NN = (((1,), (0,)), ((), ()))
_NT = (((1,), (1,)), ((), ()))
_TN = (((0,), (0,)), ((), ()))


def _dg(a, b, dims=_NN):
    return lax.dot_general(a, b, dims, preferred_element_type=F32)


def _split2(x):
    hi = x.astype(BF16)
    lo = (x - hi.astype(F32)).astype(BF16)
    return hi, lo


def _mm(a, b, dims=_NN, passes=1):
    if passes == 1:
        return _dg(a.astype(BF16), b.astype(BF16), dims)
    ah, al = _split2(a)
    bh, bl = _split2(b)
    return _dg(ah, bh, dims) + (_dg(ah, bl, dims) + _dg(al, bh, dims))


def _segsum(x, bones):
    hi, lo = _split2(x)
    cols = [slice(c0, c0 + MXU_DIM) for c0 in range(0, x.shape[1], MXU_DIM)]
    return jnp.concatenate([_dg(hi[:, s], bones) + _dg(lo[:, s], bones) for s in cols], axis=1)


def _sigmoid(x):
    return 1.0 / (1.0 + jnp.exp(-x))


def _group_norm(x, g, b, eps, bones):
    inv_n = 1.0 / HEAD_DIM
    mean = _segsum(x, bones) * inv_n
    d = x - mean
    var = _segsum(d * d, bones) * inv_n
    return d * lax.rsqrt(var + eps) * g + b


def _mod_body(c_ref, w_ref, b_ref, o_ref):
    c = c_ref[...]
    s = (c * _sigmoid(c)).astype(BF16)
    o_ref[0] = _dg(s, w_ref[0].astype(BF16)) + b_ref[0]


def _mod_call(c_all, w_ada, b_ada):
    depth, _, n3 = w_ada.shape
    rows = c_all.shape[0]
    bn = 768
    return pl.pallas_call(
        _mod_body,
        grid=(depth, n3 // bn),
        in_specs=[
            pl.BlockSpec((rows, D_MODEL), lambda l, j: (0, 0)),
            pl.BlockSpec((1, D_MODEL, bn), lambda l, j: (l, 0, j)),
            pl.BlockSpec((1, 1, bn), lambda l, j: (l, 0, j)),
        ],
        out_specs=pl.BlockSpec((1, rows, bn), lambda l, j: (l, 0, j)),
        out_shape=jax.ShapeDtypeStruct((depth, rows, n3), F32),
        compiler_params=pltpu.CompilerParams(
            dimension_semantics=("arbitrary", "arbitrary"), vmem_limit_bytes=VMEM_LIMIT),
        name="adaln_mod",
    )(c_all, w_ada, b_ada.reshape(depth, 1, n3))


def _inproj_body(time_major, x_ref, sh_ref, sc_ref, ss_ref, gpre_ref, win_ref, mu_ref,
                 w0_ref, a0_ref, wl_ref, kkp_ref, kap_ref, rkp_ref, bones_ref,
                 r_o, lw_o, km_o, v_o, kk_o, kka_o, bon_o, sgr_o, glu_o, sgc_o, hl_o, carry):
    step = pl.program_id(1)
    x = x_ref[0]
    rows = x.shape[0]
    ms = jnp.mean(x * x, axis=-1, keepdims=True)
    h = x * lax.rsqrt(ms + RMS_EPS) * gpre_ref[...]
    h = h * (1.0 + sc_ref[0]) + sh_ref[0]
    hb = h.astype(BF16)
    ur = _dg(hb, win_ref[:, :P_RWKV])

    if time_major:
        @pl.when(step == 0)
        def _():
            carry[...] = _dg(ss_ref[0].astype(BF16), win_ref[:, :P_RWKV])
        prev = carry[...]
        carry[...] = ur
        hl_o[0] = h
    else:
        @pl.when(step == 0)
        def _():
            ss8 = jnp.broadcast_to(ss_ref[0], (SUBLANES, D_MODEL)).astype(BF16)
            carry[SUBLANES - 1:SUBLANES, :] = _dg(ss8, win_ref[:, :P_RWKV])[0:1]
        carry[SUBLANES:SUBLANES + rows, :] = ur
        prev = carry[SUBLANES - 1:SUBLANES - 1 + rows, :]
        carry[SUBLANES - 1:SUBLANES, :] = ur[rows - 1:rows]
        hl_o[0] = h[rows - 1:rows]

    def conv_cols(j):
        c0 = P_RWKV + j * D_CONV
        return _dg(hb, win_ref[:, c0:c0 + D_CONV])

    urs = ur + (prev - ur) * mu_ref[...]
    r = urs[:, 0:D_RWKV]
    k = urs[:, D_RWKV:2 * D_RWKV]
    v = urs[:, 2 * D_RWKV:3 * D_RWKV]
    g_r = urs[:, 3 * D_RWKV:4 * D_RWKV]
    lora = urs[:, 4 * D_RWKV:P_RWKV]
    glu_a = conv_cols(0)
    lane = lax.broadcasted_iota(jnp.int32, lora.shape, 1)
    lora_in = jnp.where(lane < LORA, jnp.tanh(lora), lora).astype(BF16)
    wa = _dg(lora_in, wl_ref[...])
    z = -(w0_ref[...] + wa[:, :D_RWKV])
    softplus = jnp.maximum(z, 0.0) + jnp.log(1.0 + jnp.exp(-jnp.abs(z)))
    lw = -jnp.exp(-softplus - 0.5)
    a = _sigmoid(a0_ref[...] + wa[:, D_RWKV:])
    lw_o[0] = lw
    r_o[0] = r
    v_o[0] = v

    glu_b = conv_cols(1)
    bones = bones_ref[...]
    kkr = k * kkp_ref[...]
    ssq = _segsum(kkr * kkr, bones)
    kk = kkr * lax.rsqrt(jnp.maximum(ssq, 1e-24))
    kk_o[0] = kk
    kka_o[0] = kk * a

    g_c = conv_cols(2)
    km = k * (1.0 + (a - 1.0) * kap_ref[...])
    km_o[0] = km
    bon_o[0] = _segsum(r * km * rkp_ref[...], bones) * v
    sgr_o[0] = g_r * _sigmoid(g_r)

    glu_o[0] = glu_a * _sigmoid(glu_b)
    sgc_o[0] = g_c * _sigmoid(g_c)


def _inproj_call(time_major, x, shift, scale, s_shift, lw_params, bones, tile):
    n_outer, n_inner, _ = x.shape
    if time_major:
        grid = (1, n_outer)
        rows = n_inner
        x_map = lambda i, t: (t, 0, 0)
        row_map = lambda i, t: (0, 0, 0)
        row_block = (1, rows, D_MODEL)
        carry = pltpu.VMEM((rows, P_RWKV), F32)
        hl_shape = (1, rows, D_MODEL)
    else:
        rows = tile
        grid = (n_outer, n_inner // rows)
        x_map = lambda b, t: (b, t, 0)
        row_map = lambda b, t: (b, 0, 0)
        row_block = (1, 1, D_MODEL)
        carry = pltpu.VMEM((rows + SUBLANES, P_RWKV), F32)
        hl_shape = (n_outer, 1, D_MODEL)
    const = lambda i, t: (0, 0)
    g_pre, w_in, mu, w0, a0, w_lora, k_k, k_a, r_k = lw_params
    params = [g_pre, w_in, mu, w0, a0, w_lora, k_k, k_a, r_k, bones]
    out_block = pl.BlockSpec((1, rows, D_RWKV), x_map)
    out_sds = jax.ShapeDtypeStruct((n_outer, n_inner, D_RWKV), F32)
    return pl.pallas_call(
        functools.partial(_inproj_body, time_major),
        grid=grid,
        in_specs=[pl.BlockSpec((1, rows, D_MODEL), x_map)]
        + [pl.BlockSpec(row_block, row_map)] * 3
        + [pl.BlockSpec(p.shape, const) for p in params],
        out_specs=[out_block] * 10 + [pl.BlockSpec((1,) + hl_shape[1:], row_map)],
        out_shape=[out_sds] * 10 + [jax.ShapeDtypeStruct(hl_shape, F32)],
        scratch_shapes=[carry],
        compiler_params=pltpu.CompilerParams(
            dimension_semantics=("arbitrary", "arbitrary"), vmem_limit_bytes=VMEM_LIMIT),
        name="inproj_tm" if time_major else "inproj_bm",
    )(x, shift, scale, s_shift, *params)


PASS_GRAM = 1
PASS_INV = 1
PASS_STATE = 1
PASS_OUT = 1
CHUNK_SEQS = 8


def _wkv_chunk_body(r_ref, lw_ref, km_ref, v_ref, kk_ref, kka_ref, s0_ref, lt_ref,
                    y_ref, so_ref, s_scr):
    c = pl.program_id(1)
    n_seq = r_ref.shape[0]
    zeros_hh = jnp.zeros((HEAD_DIM, HEAD_DIM), F32)

    @pl.when(c == 0)
    def _():
        for i in range(n_seq):
            for p in range(N_PAIRS):
                top = jnp.concatenate([s0_ref[i, 2 * p], zeros_hh], axis=1)
                bot = jnp.concatenate([zeros_hh, s0_ref[i, 2 * p + 1]], axis=1)
                s_scr[i * N_PAIRS + p] = jnp.concatenate([top, bot], axis=0)

    lt = lt_ref[...]
    lane = lax.broadcasted_iota(jnp.int32, (CHUNK, LANES), 1)
    first = lane < HEAD_DIM
    row = lax.broadcasted_iota(jnp.int32, (LANES, LANES), 0)
    col = lax.broadcasted_iota(jnp.int32, (LANES, LANES), 1)
    same_head = (row < HEAD_DIM) == (col < HEAD_DIM)
    rt_i = row & (HEAD_DIM - 1)
    ct_i = col & (HEAD_DIM - 1)
    tri_strict = same_head & (ct_i < rt_i)
    tri_incl = same_head & (ct_i <= rt_i)
    eye = (row == col).astype(F32)

    def stack(xp):
        return jnp.concatenate([jnp.where(first, xp, 0.0), jnp.where(first, 0.0, xp)], axis=0)

    x2, y2, vs, hk, p_last = [], [], [], [], []
    for i in range(n_seq):
        lw = lw_ref[i]
        l_hi = lw.astype(BF16)
        l_r1 = lw - l_hi.astype(F32)
        l_mid = l_r1.astype(BF16)
        l_lo = (l_r1 - l_mid.astype(F32)).astype(BF16)
        cum = _dg(lt, l_hi) + (_dg(lt, l_mid) + _dg(lt, l_lo))
        cum_last = cum[CHUNK - 1:CHUNK, :]
        p_inv = jnp.exp(-cum)
        p_end = jnp.exp(cum_last - cum)
        at_all = -kk_ref[i] * jnp.exp(cum - lw)
        rt_all = r_ref[i] * jnp.exp(cum)
        kka = kka_ref[i]
        km = km_ref[i]
        bt_all = kka * p_inv
        kt_all = km * p_inv
        bh_all = kka * p_end
        kh_all = km * p_end
        v_all = v_ref[i]
        pl_all = jnp.exp(cum_last)
        for p in range(N_PAIRS):
            s = slice(LANES * p, LANES * (p + 1))
            x2.append(jnp.concatenate([stack(at_all[:, s]), stack(rt_all[:, s])], axis=0))
            y2.append(jnp.concatenate([stack(bt_all[:, s]), stack(kt_all[:, s])], axis=0))
            vs.append(stack(v_all[:, s]))
            hk.append(jnp.concatenate([stack(bh_all[:, s]), stack(kh_all[:, s])], axis=0))
            p_last.append(pl_all[:, s])

    units = range(n_seq * N_PAIRS)
    g = [_mm(x2[u], y2[u], _NT, PASS_GRAM) for u in units]
    n_pow = [jnp.where(tri_strict, g[u][:LANES, :LANES], 0.0) for u in units]
    a_ak = [jnp.where(tri_strict, g[u][:LANES, LANES:], 0.0) for u in units]
    a_r = [jnp.concatenate([jnp.where(tri_incl, g[u][LANES:, :LANES], 0.0),
                            jnp.where(tri_incl, g[u][LANES:, LANES:], 0.0)], axis=1) for u in units]

    t_inv = [eye + n_pow[u] for u in units]
    span = 2
    while span < CHUNK:
        n_pow = [_mm(n_pow[u], n_pow[u], _NN, PASS_INV) for u in units]
        t_inv = [t_inv[u] + _mm(t_inv[u], n_pow[u], _NN, PASS_INV) for u in units]
        span *= 2

    s_bd = [s_scr[u] for u in units]
    xs = [_mm(x2[u], s_bd[u], _NT, PASS_STATE) for u in units]
    akv = [_mm(a_ak[u], vs[u], _NN, PASS_STATE) for u in units]
    z_mat = [_mm(t_inv[u], xs[u][:LANES] + akv[u], _NN, PASS_STATE) for u in units]
    zv = [jnp.concatenate([z_mat[u], vs[u]], axis=0) for u in units]
    y_bd = [xs[u][LANES:] + _mm(a_r[u], zv[u], _NN, PASS_OUT) for u in units]
    s_new = [s_bd[u] * p_last[u] + _mm(zv[u], hk[u], _TN, PASS_STATE) for u in units]
    for u in units:
        i, p = divmod(u, N_PAIRS)
        y_ref[i, :, LANES * p:LANES * (p + 1)] = y_bd[u][:HEAD_DIM] + y_bd[u][HEAD_DIM:]
        s_scr[u] = s_new[u]
        so_ref[i, 2 * p] = s_new[u][:HEAD_DIM, :HEAD_DIM]
        so_ref[i, 2 * p + 1] = s_new[u][HEAD_DIM:, HEAD_DIM:]


def _wkv_chunk_call(r, lw, km, v, kk, kka, s0, lt):
    n_b, n_t, _ = r.shape
    nq = CHUNK_SEQS
    seq_block = pl.BlockSpec((nq, CHUNK, D_RWKV), lambda b, c: (b, c, 0))
    st_block = pl.BlockSpec((nq, N_HEADS, HEAD_DIM, HEAD_DIM), lambda b, c: (b, 0, 0, 0))
    return pl.pallas_call(
        _wkv_chunk_body,
        grid=(n_b // nq, n_t // CHUNK),
        in_specs=[seq_block] * 6 + [st_block, pl.BlockSpec(lt.shape, lambda b, c: (0, 0))],
        out_specs=[seq_block, st_block],
        out_shape=[jax.ShapeDtypeStruct(r.shape, F32), jax.ShapeDtypeStruct(s0.shape, F32)],
        scratch_shapes=[pltpu.VMEM((nq * N_PAIRS, LANES, LANES), F32)],
        compiler_params=pltpu.CompilerParams(
            dimension_semantics=("arbitrary", "arbitrary"), vmem_limit_bytes=VMEM_LIMIT),
        name="wkv_chunk",
    )(r, lw, km, v, kk, kka, s0, lt)


SHORT_T = 4
SHORT_SEQS = LANES // (2 * SHORT_T)


def _wkv_short_body(r_ref, lw_ref, km_ref, v_ref, kk_ref, kka_ref, s_ref, y_ref, so_ref):
    n_seq = s_ref.shape[0]
    grp = 2 * SHORT_T
    row = lax.broadcasted_iota(jnp.int32, (LANES, LANES), 0)
    col = lax.broadcasted_iota(jnp.int32, (LANES, LANES), 1)
    same_blk = (row // SHORT_T) == (col // SHORT_T)
    t_row = row % SHORT_T
    t_col = col % SHORT_T
    strict = same_blk & (t_col < t_row)
    incl = same_blk & (t_col <= t_row)
    eye = (row == col).astype(F32)
    sum_mat = jnp.concatenate([incl.astype(BF16), same_blk.astype(BF16)], axis=0)
    own = ((row // SHORT_T) % 2) == (col // HEAD_DIM)
    zeros_hh = jnp.zeros((HEAD_DIM, HEAD_DIM), F32)

    def rows_of(ref, p):
        return ref[:, :, LANES * p:LANES * (p + 1)].reshape(n_seq * grp, LANES)

    def mask(x):
        return jnp.where(own, x, 0.0)

    pairs = range(N_PAIRS)
    x2, y2, vs, bh, kh, p_last = [], [], [], [], [], []
    for p in pairs:
        lw = rows_of(lw_ref, p)
        l_hi = lw.astype(BF16)
        l_r1 = lw - l_hi.astype(F32)
        l_mid = l_r1.astype(BF16)
        l_lo = (l_r1 - l_mid.astype(F32)).astype(BF16)
        sums = _dg(sum_mat, l_hi) + (_dg(sum_mat, l_mid) + _dg(sum_mat, l_lo))
        cum = sums[:LANES]
        tot = sums[LANES:]
        p_inv = jnp.exp(-cum)
        p_end = jnp.exp(tot - cum)
        kka = rows_of(kka_ref, p)
        km = rows_of(km_ref, p)
        x2.append(jnp.concatenate([mask(-rows_of(kk_ref, p) * jnp.exp(cum - lw)),
                                   mask(rows_of(r_ref, p) * jnp.exp(cum))], axis=0))
        y2.append(jnp.concatenate([mask(kka * p_inv), mask(km * p_inv)], axis=0))
        vs.append(mask(rows_of(v_ref, p)))
        bh.append(mask(kka * p_end))
        kh.append(mask(km * p_end))
        p_last.append(jnp.exp(tot))

    g = [_mm(x2[p], y2[p], _NT, PASS_GRAM) for p in pairs]
    n_mat = [jnp.where(strict, g[p][:LANES, :LANES], 0.0) for p in pairs]
    a_ak = [jnp.where(strict, g[p][:LANES, LANES:], 0.0) for p in pairs]
    a_r = [jnp.concatenate([jnp.where(incl, g[p][LANES:, :LANES], 0.0),
                            jnp.where(incl, g[p][LANES:, LANES:], 0.0)], axis=1) for p in pairs]
    n_sq = [_mm(n_mat[p], n_mat[p], _NN, PASS_INV) for p in pairs]
    t_inv = [(eye + n_mat[p]) + _mm(eye + n_mat[p], n_sq[p], _NN, PASS_INV) for p in pairs]
    akv = [_mm(a_ak[p], vs[p], _NN, PASS_STATE) for p in pairs]

    def seq_rows(x, i):
        return x[grp * i:grp * (i + 1)]

    s_bd, xa_s, xr_s = [], [], []
    for p in pairs:
        s_p, xa_p, xr_p = [], [], []
        for i in range(n_seq):
            top = jnp.concatenate([s_ref[i, 2 * p], zeros_hh], axis=1)
            bot = jnp.concatenate([zeros_hh, s_ref[i, 2 * p + 1]], axis=1)
            s_i = jnp.concatenate([top, bot], axis=0)
            x_i = jnp.concatenate([seq_rows(x2[p][:LANES], i), seq_rows(x2[p][LANES:], i)], axis=0)
            xs_i = _mm(x_i, s_i, _NT, PASS_STATE)
            s_p.append(s_i)
            xa_p.append(xs_i[:grp])
            xr_p.append(xs_i[grp:])
        s_bd.append(s_p)
        xa_s.append(jnp.concatenate(xa_p, axis=0))
        xr_s.append(jnp.concatenate(xr_p, axis=0))

    z_mat = [_mm(t_inv[p], xa_s[p] + akv[p], _NN, PASS_STATE) for p in pairs]
    zv = [jnp.concatenate([z_mat[p], vs[p]], axis=0) for p in pairs]
    y_bd = [xr_s[p] + _mm(a_r[p], zv[p], _NN, PASS_OUT) for p in pairs]
    for p in pairs:
        y3 = y_bd[p].reshape(n_seq, grp, LANES)
        y_ref[:, :, LANES * p:LANES * (p + 1)] = y3 + jnp.concatenate(
            [y3[:, SHORT_T:], y3[:, :SHORT_T]], axis=1)
        for i in range(n_seq):
            zv_i = jnp.concatenate([seq_rows(z_mat[p], i), seq_rows(vs[p], i)], axis=0)
            hk_i = jnp.concatenate([seq_rows(bh[p], i), seq_rows(kh[p], i)], axis=0)
            s_new = (s_bd[p][i] * p_last[p][grp * i:grp * i + 1]
                     + _mm(zv_i, hk_i, _TN, PASS_STATE))
            so_ref[i, 2 * p] = s_new[:HEAD_DIM, :HEAD_DIM]
            so_ref[i, 2 * p + 1] = s_new[HEAD_DIM:, HEAD_DIM:]


def _wkv_short_call(r, lw, km, v, kk, kka, s0):
    n_b = r.shape[0]
    nq = SHORT_SEQS
    seq_block = pl.BlockSpec((nq, 2 * SHORT_T, D_RWKV), lambda i: (i, 0, 0))
    st_block = pl.BlockSpec((nq, N_HEADS, HEAD_DIM, HEAD_DIM), lambda i: (i, 0, 0, 0))
    return pl.pallas_call(
        _wkv_short_body,
        grid=(n_b // nq,),
        in_specs=[seq_block] * 6 + [st_block],
        out_specs=[seq_block, st_block],
        out_shape=[jax.ShapeDtypeStruct(r.shape, F32), jax.ShapeDtypeStruct(s0.shape, F32)],
        compiler_params=pltpu.CompilerParams(
            dimension_semantics=("arbitrary",), vmem_limit_bytes=VMEM_LIMIT),
        name="wkv_short",
    )(r, lw, km, v, kk, kka, s0)


CONV_ROW_BLOCK = 32
HIST_PAD = 32


def _merge_and_residual(y, bonus, sgr, conv, sgc, x, gate, gnr_g, gnr_b, gnc_g, gnc_b,
                        wout_ref, gpost, bones):
    y_r = (_group_norm(y, gnr_g, gnr_b, RWKV_GN_EPS, bones) + bonus) * sgr
    cn = _group_norm(conv, gnc_g, gnc_b, CONV_GN_EPS, bones)
    y_c = cn * _sigmoid(cn) * sgc
    mix = _dg(y_r.astype(BF16), wout_ref[0:D_RWKV, :]) + _dg(y_c.astype(BF16), wout_ref[D_RWKV:, :])
    ms = jnp.mean(mix * mix, axis=-1, keepdims=True)
    return x + gate * (mix * lax.rsqrt(ms + RMS_EPS) * gpost)


def _post_bm_body(y_ref, bon_ref, sgr_ref, glu_ref, sgc_ref, x_ref, gate_ref, sconv_ref,
                  gnrg_ref, gnrb_ref, wdw_ref, bdw_ref, gncg_ref, gncb_ref, wout_ref, gpost_ref,
                  bones_ref, xo_ref, buf, conv):
    step = pl.program_id(1)
    rows = y_ref.shape[1]
    off = HIST_PAD - HIST
    shifted_rows = rows + HIST_PAD - SUBLANES

    @pl.when(step == 0)
    def _():
        buf[0, 0:off, :] = jnp.zeros((off, D_CONV), F32)
        buf[0, off:HIST_PAD, :] = sconv_ref[0]

    buf[0, HIST_PAD:HIST_PAD + rows, :] = glu_ref[0]
    for q in range(1, SUBLANES):
        buf[q, 0:shifted_rows, :] = buf[0, q:q + shifted_rows, :]
    for r0 in range(0, rows, CONV_ROW_BLOCK):
        acc = jnp.broadcast_to(bdw_ref[...], (CONV_ROW_BLOCK, D_CONV))
        for j in range(CONV_W):
            q = (j + off) % SUBLANES
            base = r0 + j + off - q
            acc = acc + wdw_ref[j:j + 1, :] * buf[q, base:base + CONV_ROW_BLOCK, :]
        conv[r0:r0 + CONV_ROW_BLOCK, :] = acc
    buf[0, off:HIST_PAD, :] = buf[0, rows + off:rows + HIST_PAD, :]

    xo_ref[0] = _merge_and_residual(
        y_ref[0], bon_ref[0], sgr_ref[0], conv[...], sgc_ref[0], x_ref[0], gate_ref[0],
        gnrg_ref[...], gnrb_ref[...], gncg_ref[...], gncb_ref[...], wout_ref, gpost_ref[...],
        bones_ref[...])


def _post_tm_body(y_ref, bon_ref, sgr_ref, glu_ref, sgc_ref, x_ref, gate_ref, sconv_ref,
                  gnrg_ref, gnrb_ref, wdw_ref, bdw_ref, gncg_ref, gncb_ref, wout_ref, gpost_ref,
                  bones_ref, xo_ref):
    n_t, rows, _ = y_ref.shape
    for t in range(n_t):
        acc = jnp.broadcast_to(bdw_ref[...], (rows, D_CONV))
        for j in range(CONV_W):
            i = t + j
            src = sconv_ref[i] if i < HIST else glu_ref[i - HIST]
            acc = acc + wdw_ref[j:j + 1, :] * src
        xo_ref[t] = _merge_and_residual(
            y_ref[t], bon_ref[t], sgr_ref[t], acc, sgc_ref[t], x_ref[t], gate_ref[0],
            gnrg_ref[...], gnrb_ref[...], gncg_ref[...], gncb_ref[...], wout_ref, gpost_ref[...],
            bones_ref[...])


def _post_call(time_major, y, bonus, sgr, glu, sgc, x, gate, s_conv, post_params, bones, tile):
    n_outer, n_inner, _ = x.shape
    const = lambda i, t: (0, 0)
    if time_major:
        rows = n_inner
        grid = (1, 1)
        lead = n_outer
        x_map = lambda i, t: (0, 0, 0)
        gate_spec = pl.BlockSpec((1, rows, D_MODEL), lambda i, t: (0, 0, 0))
        sconv_spec = pl.BlockSpec(s_conv.shape, lambda i, t: (0, 0, 0))
        scratch = []
        body = _post_tm_body
    else:
        rows = tile
        grid = (n_outer, n_inner // rows)
        lead = 1
        x_map = lambda b, t: (b, t, 0)
        gate_spec = pl.BlockSpec((1, 1, D_MODEL), lambda b, t: (b, 0, 0))
        sconv_spec = pl.BlockSpec((1, HIST, D_CONV), lambda b, t: (b, 0, 0))
        scratch = [pltpu.VMEM((SUBLANES, HIST_PAD + rows, D_CONV), F32),
                   pltpu.VMEM((rows, D_CONV), F32)]
        body = _post_bm_body
    params = list(post_params) + [bones]
    half = pl.BlockSpec((lead, rows, D_RWKV), x_map)
    full = pl.BlockSpec((lead, rows, D_MODEL), x_map)
    return pl.pallas_call(
        body,
        grid=grid,
        in_specs=[half] * 5 + [full, gate_spec, sconv_spec]
        + [pl.BlockSpec(p.shape, const) for p in params],
        out_specs=full,
        out_shape=jax.ShapeDtypeStruct(x.shape, F32),
        scratch_shapes=scratch,
        compiler_params=pltpu.CompilerParams(
            dimension_semantics=("arbitrary", "arbitrary"), vmem_limit_bytes=VMEM_LIMIT),
        name="post_tm" if time_major else "post_bm",
    )(y, bonus, sgr, glu, sgc, x, gate, s_conv, *params)


PROMPT_TILE = 256


def _block_ones():
    i = lax.broadcasted_iota(jnp.int32, (MXU_DIM, MXU_DIM), 0) // HEAD_DIM
    j = lax.broadcasted_iota(jnp.int32, (MXU_DIM, MXU_DIM), 1) // HEAD_DIM
    return (i == j).astype(BF16)


def _lower_tri():
    i = lax.broadcasted_iota(jnp.int32, (CHUNK, CHUNK), 0)
    j = lax.broadcasted_iota(jnp.int32, (CHUNK, CHUNK), 1)
    return (j <= i).astype(BF16)


def kernel(x_prompt, x_sample, c_prompt, c_sample, state_shift, state_wkv, state_conv, w_ada, b_ada, g_pre, g_post, w_in, mu, w0, w_up, a0, a_up, k_k, k_a, r_k, gn_r_g, gn_r_b, w_dw, b_dw, gn_c_g, gn_c_b, w_out):
    depth = w_in.shape[0]
    n_p, t_p, _ = x_prompt.shape
    n_s, t_s, _ = x_sample.shape
    bones = _block_ones()
    lt = _lower_tri()

    mod = _mod_call(jnp.concatenate([c_prompt, c_sample], axis=0), w_ada, b_ada)

    xp = x_prompt
    xs = jnp.transpose(x_sample, (1, 0, 2))
    zeros_shift = jnp.zeros((n_p, 1, D_MODEL), F32)
    zeros_wkv = jnp.zeros((n_p, N_HEADS, HEAD_DIM, HEAD_DIM), F32)
    zeros_conv = jnp.zeros((n_p, HIST, D_CONV), F32)

    def dup_rows(a):
        a = jnp.transpose(a, (1, 0, 2))
        return jnp.concatenate([a, a], axis=1)

    shifts_p, wkvs_p, convs_p, shifts_s, wkvs_s, convs_s = [], [], [], [], [], []
    for l in range(depth):
        row = lambda a: a[l].reshape(1, -1)
        zl = jnp.zeros((LORA, D_RWKV), F32)
        w_lora = jnp.concatenate(
            [jnp.concatenate([w_up[l], zl], axis=1), jnp.concatenate([zl, a_up[l]], axis=1)],
            axis=0).astype(BF16)
        in_params = (row(g_pre), w_in[l].astype(BF16), row(mu), row(w0), row(a0), w_lora,
                     row(k_k), row(k_a), row(r_k))
        post_params = (row(gn_r_g), row(gn_r_b), w_dw[l], row(b_dw), row(gn_c_g), row(gn_c_b),
                       w_out[l].astype(BF16), row(g_post))

        m = mod[l, :n_p]
        sh, sc, gt = (m[:, None, i * D_MODEL:(i + 1) * D_MODEL] for i in range(3))
        (r, lw, km, v, kk, kka, bonus, sgr, glu, sgc, hl) = _inproj_call(
            False, xp, sh, sc, zeros_shift, in_params, bones, PROMPT_TILE)
        y, s_new = _wkv_chunk_call(r, lw, km, v, kk, kka, zeros_wkv, lt)
        xp = _post_call(False, y, bonus, sgr, glu, sgc, xp, gt, zeros_conv, post_params, bones,
                        PROMPT_TILE)
        shifts_p.append(hl[:, 0])
        wkvs_p.append(s_new)
        convs_p.append(glu[:, -HIST:] if t_p >= HIST
                       else jnp.concatenate([zeros_conv, glu], axis=1)[:, -HIST:])

        m = mod[l, n_p:]
        sh, sc, gt = (m[None, :, i * D_MODEL:(i + 1) * D_MODEL] for i in range(3))
        (r, lw, km, v, kk, kka, bonus, sgr, glu, sgc, hl) = _inproj_call(
            True, xs, sh, sc, state_shift[l][None], in_params, bones, None)
        y, s_new = _wkv_short_call(*(dup_rows(a) for a in (r, lw, km, v, kk, kka)), state_wkv[l])
        y = jnp.transpose(y[:, :SHORT_T], (1, 0, 2))
        xs = _post_call(True, y, bonus, sgr, glu, sgc, xs, gt,
                        jnp.transpose(state_conv[l], (1, 0, 2)), post_params, bones, None)
        shifts_s.append(hl[0])
        wkvs_s.append(s_new)
        convs_s.append(jnp.concatenate(
            [state_conv[l][:, t_s:], jnp.transpose(glu, (1, 0, 2))], axis=1)[:, -HIST:])

    return (xp, jnp.transpose(xs, (1, 0, 2)),
            jnp.stack(shifts_p), jnp.stack(wkvs_p), jnp.stack(convs_p),
            jnp.stack(shifts_s), jnp.stack(wkvs_s), jnp.stack(convs_s))
```

```python
import functools

import jax
import jax.numpy as jnp
from jax import lax
from jax.experimental import pallas as pl
from jax.experimental.pallas import tpu as pltpu

F32 = jnp.float32
BF16 = jnp.bfloat16

D_MODEL = 1024
D_RWKV = 512
D_CONV = 512
HEAD_DIM = 64
N_HEADS = 8
N_PAIRS = N_HEADS // 2
LORA = 64
P_RWKV = 4 * D_RWKV + 2 * LORA
P_IN = P_RWKV + 3 * D_CONV
CONV_W = 31
HIST = CONV_W - 1
RMS_EPS = 1e-6
RWKV_GN_EPS = 64e-5
CONV_GN_EPS = 1e-5

LANES = 128
SUBLANES = 8
MXU_DIM = 256
CHUNK = 64
VMEM_LIMIT = 56 * 1024 * 1024

_NN = (((1,), (0,)), ((), ()))
_NT = (((1,), (1,)), ((), ()))
_TN = (((0,), (0,)), ((), ()))


def _dg(a, b, dims=_NN):
    return lax.dot_general(a, b, dims, preferred_element_type=F32)


def _split2(x):
    hi = x.astype(BF16)
    lo = (x - hi.astype(F32)).astype(BF16)
    return hi, lo


def _mm(a, b, dims=_NN, passes=1):
    if passes == 1:
        return _dg(a.astype(BF16), b.astype(BF16), dims)
    ah, al = _split2(a)
    bh, bl = _split2(b)
    return _dg(ah, bh, dims) + (_dg(ah, bl, dims) + _dg(al, bh, dims))


def _segsum(x, bones):
    hi, lo = _split2(x)
    cols = [slice(c0, c0 + MXU_DIM) for c0 in range(0, x.shape[1], MXU_DIM)]
    return jnp.concatenate([_dg(hi[:, s], bones) + _dg(lo[:, s], bones) for s in cols], axis=1)


def _sigmoid(x):
    return 1.0 / (1.0 + jnp.exp(-x))


def _group_norm(x, g, b, eps, bones):
    inv_n = 1.0 / HEAD_DIM
    mean = _segsum(x, bones) * inv_n
    d = x - mean
    var = _segsum(d * d, bones) * inv_n
    return d * lax.rsqrt(var + eps) * g + b


def _mod_body(cp_ref, cs_ref, w_ref, b_ref, op_ref, os_ref):
    w = w_ref[...].astype(BF16)
    for c_ref, o_ref in ((cp_ref, op_ref), (cs_ref, os_ref)):
        c = c_ref[...]
        o_ref[...] = _dg((c * _sigmoid(c)).astype(BF16), w) + b_ref[...]


def _mod_call(c_prompt, c_sample, w_ada, b_ada):
    depth, _, n3 = w_ada.shape
    bn = 768
    layer_cols = lambda l, j: (l, 0, j)
    whole = lambda a: pl.BlockSpec(a.shape, lambda l, j: (0, 0))
    out = lambda a: pl.BlockSpec((None, a.shape[0], bn), layer_cols)
    return pl.pallas_call(
        _mod_body,
        grid=(depth, n3 // bn),
        in_specs=[whole(c_prompt), whole(c_sample),
                  pl.BlockSpec((None, D_MODEL, bn), layer_cols),
                  pl.BlockSpec((None, 1, bn), layer_cols)],
        out_specs=[out(c_prompt), out(c_sample)],
        out_shape=[jax.ShapeDtypeStruct((depth, c.shape[0], n3), F32) for c in (c_prompt, c_sample)],
        compiler_params=pltpu.CompilerParams(
            dimension_semantics=("arbitrary", "arbitrary"), vmem_limit_bytes=VMEM_LIMIT),
        name="adaln_mod",
    )(c_prompt, c_sample, w_ada, b_ada.reshape(depth, 1, n3))


def _layer_spec(a, l, n_grid):
    idx = (l,) + (0,) * (a.ndim - 1)
    index_map = (lambda i: idx) if n_grid == 1 else (lambda i, t: idx)
    return pl.BlockSpec((None,) + a.shape[1:], index_map)


def _inproj_body(time_major, x_ref, sh_ref, sc_ref, ss_ref, gpre_ref, win_ref, mu_ref,
                 w0_ref, a0_ref, wl_ref, kkp_ref, kap_ref, rkp_ref, bones_ref,
                 r_o, lw_o, km_o, v_o, kk_o, kka_o, bon_o, sgr_o, glu_o, sgc_o, hl_o, carry):
    step = pl.program_id(1)
    x = x_ref[0]
    rows = x.shape[0]
    ms = jnp.mean(x * x, axis=-1, keepdims=True)
    h = x * lax.rsqrt(ms + RMS_EPS) * gpre_ref[...]
    h = h * (1.0 + sc_ref[...]) + sh_ref[...]
    hb = h.astype(BF16)
    ur = _dg(hb, win_ref[:, :P_RWKV])

    if time_major:
        @pl.when(step == 0)
        def _():
            carry[...] = _dg(ss_ref[...].astype(BF16), win_ref[:, :P_RWKV])
        prev = carry[...]
        carry[...] = ur
        hl_o[...] = h
    else:
        @pl.when(step == 0)
        def _():
            ss8 = jnp.broadcast_to(ss_ref[...], (SUBLANES, D_MODEL)).astype(BF16)
            carry[SUBLANES - 1:SUBLANES, :] = _dg(ss8, win_ref[:, :P_RWKV])[0:1]
        carry[SUBLANES:SUBLANES + rows, :] = ur
        prev = carry[SUBLANES - 1:SUBLANES - 1 + rows, :]
        carry[SUBLANES - 1:SUBLANES, :] = ur[rows - 1:rows]
        hl_o[...] = h[rows - 1:rows]

    def conv_cols(j):
        c0 = P_RWKV + j * D_CONV
        return _dg(hb, win_ref[:, c0:c0 + D_CONV])

    urs = ur + (prev - ur) * mu_ref[...]
    r = urs[:, 0:D_RWKV]
    k = urs[:, D_RWKV:2 * D_RWKV]
    v = urs[:, 2 * D_RWKV:3 * D_RWKV]
    g_r = urs[:, 3 * D_RWKV:4 * D_RWKV]
    lora = urs[:, 4 * D_RWKV:P_RWKV]
    glu_a = conv_cols(0)
    lane = lax.broadcasted_iota(jnp.int32, lora.shape, 1)
    lora_in = jnp.where(lane < LORA, jnp.tanh(lora), lora).astype(BF16)
    wa = _dg(lora_in, wl_ref[...])
    z = -(w0_ref[...] + wa[:, :D_RWKV])
    softplus = jnp.maximum(z, 0.0) + jnp.log(1.0 + jnp.exp(-jnp.abs(z)))
    lw = -jnp.exp(-softplus - 0.5)
    a = _sigmoid(a0_ref[...] + wa[:, D_RWKV:])
    lw_o[0] = lw
    r_o[0] = r
    v_o[0] = v

    glu_b = conv_cols(1)
    bones = bones_ref[...]
    kkr = k * kkp_ref[...]
    ssq = _segsum(kkr * kkr, bones)
    kk = kkr * lax.rsqrt(jnp.maximum(ssq, 1e-24))
    kk_o[0] = kk
    kka_o[0] = kk * a

    g_c = conv_cols(2)
    km = k * (1.0 + (a - 1.0) * kap_ref[...])
    km_o[0] = km
    bon_o[0] = _segsum(r * km * rkp_ref[...], bones) * v
    sgr_o[0] = g_r * _sigmoid(g_r)

    glu_o[0] = glu_a * _sigmoid(glu_b)
    sgc_o[0] = g_c * _sigmoid(g_c)


def _inproj_call(time_major, l, x, mod, s_shift, layer_params, bones, tile):
    n_outer, n_inner, _ = x.shape
    if time_major:
        grid = (1, n_outer)
        rows = n_inner
        x_map = lambda i, t: (t, 0, 0)
        mod_spec = lambda j: pl.BlockSpec((None, rows, D_MODEL), lambda i, t: (l, 0, j))
        shift_spec = _layer_spec(s_shift, l, 2)
        carry = pltpu.VMEM((rows, P_RWKV), F32)
        hl_shape = (rows, D_MODEL)
        hl_spec = pl.BlockSpec(hl_shape, lambda i, t: (0, 0))
    else:
        rows = tile
        grid = (n_outer, n_inner // rows)
        x_map = lambda b, t: (b, t, 0)
        mod_spec = lambda j: pl.BlockSpec((None, None, 1, D_MODEL), lambda b, t: (l, b, 0, j))
        shift_spec = pl.BlockSpec((None, 1, D_MODEL), lambda b, t: (b, 0, 0))
        carry = pltpu.VMEM((rows + SUBLANES, P_RWKV), F32)
        hl_shape = (n_outer, 1, D_MODEL)
        hl_spec = pl.BlockSpec((None, 1, D_MODEL), lambda b, t: (b, 0, 0))
    out_block = pl.BlockSpec((1, rows, D_RWKV), x_map)
    out_sds = jax.ShapeDtypeStruct((n_outer, n_inner, D_RWKV), F32)
    return pl.pallas_call(
        functools.partial(_inproj_body, time_major),
        grid=grid,
        in_specs=[pl.BlockSpec((1, rows, D_MODEL), x_map), mod_spec(0), mod_spec(1), shift_spec]
        + [_layer_spec(p, l, 2) for p in layer_params]
        + [pl.BlockSpec(bones.shape, lambda i, t: (0, 0))],
        out_specs=[out_block] * 10 + [hl_spec],
        out_shape=[out_sds] * 10 + [jax.ShapeDtypeStruct(hl_shape, F32)],
        scratch_shapes=[carry],
        compiler_params=pltpu.CompilerParams(
            dimension_semantics=("arbitrary", "arbitrary"), vmem_limit_bytes=VMEM_LIMIT),
        name="inproj_tm" if time_major else "inproj_bm",
    )(x, mod, mod, s_shift, *layer_params, bones)


PASS_GRAM = 1
PASS_INV = 1
PASS_STATE = 1
PASS_OUT = 1
CHUNK_SEQS = 8


def _wkv_chunk_body(r_ref, lw_ref, km_ref, v_ref, kk_ref, kka_ref, s0_ref, lt_ref,
                    y_ref, so_ref, s_scr):
    c = pl.program_id(1)
    n_seq = r_ref.shape[0]
    zeros_hh = jnp.zeros((HEAD_DIM, HEAD_DIM), F32)

    @pl.when(c == 0)
    def _():
        for i in range(n_seq):
            for p in range(N_PAIRS):
                top = jnp.concatenate([s0_ref[i, 2 * p], zeros_hh], axis=1)
                bot = jnp.concatenate([zeros_hh, s0_ref[i, 2 * p + 1]], axis=1)
                s_scr[i * N_PAIRS + p] = jnp.concatenate([top, bot], axis=0)

    lt = lt_ref[...]
    lane = lax.broadcasted_iota(jnp.int32, (CHUNK, LANES), 1)
    first = lane < HEAD_DIM
    row = lax.broadcasted_iota(jnp.int32, (LANES, LANES), 0)
    col = lax.broadcasted_iota(jnp.int32, (LANES, LANES), 1)
    same_head = (row < HEAD_DIM) == (col < HEAD_DIM)
    rt_i = row & (HEAD_DIM - 1)
    ct_i = col & (HEAD_DIM - 1)
    tri_strict = same_head & (ct_i < rt_i)
    tri_incl = same_head & (ct_i <= rt_i)
    eye = (row == col).astype(F32)

    def stack(xp):
        return jnp.concatenate([jnp.where(first, xp, 0.0), jnp.where(first, 0.0, xp)], axis=0)

    x2, y2, vs, hk, p_last = [], [], [], [], []
    for i in range(n_seq):
        lw = lw_ref[i]
        l_hi = lw.astype(BF16)
        l_r1 = lw - l_hi.astype(F32)
        l_mid = l_r1.astype(BF16)
        l_lo = (l_r1 - l_mid.astype(F32)).astype(BF16)
        cum = _dg(lt, l_hi) + (_dg(lt, l_mid) + _dg(lt, l_lo))
        cum_last = cum[CHUNK - 1:CHUNK, :]
        p_inv = jnp.exp(-cum)
        p_end = jnp.exp(cum_last - cum)
        at_all = -kk_ref[i] * jnp.exp(cum - lw)
        rt_all = r_ref[i] * jnp.exp(cum)
        kka = kka_ref[i]
        km = km_ref[i]
        bt_all = kka * p_inv
        kt_all = km * p_inv
        bh_all = kka * p_end
        kh_all = km * p_end
        v_all = v_ref[i]
        pl_all = jnp.exp(cum_last)
        for p in range(N_PAIRS):
            s = slice(LANES * p, LANES * (p + 1))
            x2.append(jnp.concatenate([stack(at_all[:, s]), stack(rt_all[:, s])], axis=0))
            y2.append(jnp.concatenate([stack(bt_all[:, s]), stack(kt_all[:, s])], axis=0))
            vs.append(stack(v_all[:, s]))
            hk.append(jnp.concatenate([stack(bh_all[:, s]), stack(kh_all[:, s])], axis=0))
            p_last.append(pl_all[:, s])

    units = range(n_seq * N_PAIRS)
    g = [_mm(x2[u], y2[u], _NT, PASS_GRAM) for u in units]
    n_pow = [jnp.where(tri_strict, g[u][:LANES, :LANES], 0.0) for u in units]
    a_ak = [jnp.where(tri_strict, g[u][:LANES, LANES:], 0.0) for u in units]
    a_r = [jnp.concatenate([jnp.where(tri_incl, g[u][LANES:, :LANES], 0.0),
                            jnp.where(tri_incl, g[u][LANES:, LANES:], 0.0)], axis=1) for u in units]

    t_inv = [eye + n_pow[u] for u in units]
    span = 2
    while span < CHUNK:
        n_pow = [_mm(n_pow[u], n_pow[u], _NN, PASS_INV) for u in units]
        t_inv = [t_inv[u] + _mm(t_inv[u], n_pow[u], _NN, PASS_INV) for u in units]
        span *= 2

    s_bd = [s_scr[u] for u in units]
    xs = [_mm(x2[u], s_bd[u], _NT, PASS_STATE) for u in units]
    akv = [_mm(a_ak[u], vs[u], _NN, PASS_STATE) for u in units]
    z_mat = [_mm(t_inv[u], xs[u][:LANES] + akv[u], _NN, PASS_STATE) for u in units]
    zv = [jnp.concatenate([z_mat[u], vs[u]], axis=0) for u in units]
    y_bd = [xs[u][LANES:] + _mm(a_r[u], zv[u], _NN, PASS_OUT) for u in units]
    s_new = [s_bd[u] * p_last[u] + _mm(zv[u], hk[u], _TN, PASS_STATE) for u in units]
    for u in units:
        i, p = divmod(u, N_PAIRS)
        y_ref[i, :, LANES * p:LANES * (p + 1)] = y_bd[u][:HEAD_DIM] + y_bd[u][HEAD_DIM:]
        s_scr[u] = s_new[u]
        so_ref[i, 2 * p] = s_new[u][:HEAD_DIM, :HEAD_DIM]
        so_ref[i, 2 * p + 1] = s_new[u][HEAD_DIM:, HEAD_DIM:]


def _wkv_chunk_call(r, lw, km, v, kk, kka, s0, lt):
    n_b, n_t, _ = r.shape
    nq = CHUNK_SEQS
    seq_block = pl.BlockSpec((nq, CHUNK, D_RWKV), lambda b, c: (b, c, 0))
    st_block = pl.BlockSpec((nq, N_HEADS, HEAD_DIM, HEAD_DIM), lambda b, c: (b, 0, 0, 0))
    return pl.pallas_call(
        _wkv_chunk_body,
        grid=(n_b // nq, n_t // CHUNK),
        in_specs=[seq_block] * 6 + [st_block, pl.BlockSpec(lt.shape, lambda b, c: (0, 0))],
        out_specs=[seq_block, st_block],
        out_shape=[jax.ShapeDtypeStruct(r.shape, F32), jax.ShapeDtypeStruct(s0.shape, F32)],
        scratch_shapes=[pltpu.VMEM((nq * N_PAIRS, LANES, LANES), F32)],
        compiler_params=pltpu.CompilerParams(
            dimension_semantics=("arbitrary", "arbitrary"), vmem_limit_bytes=VMEM_LIMIT),
        name="wkv_chunk",
    )(r, lw, km, v, kk, kka, s0, lt)


SHORT_T = 4
SHORT_SEQS = LANES // (2 * SHORT_T)


def _wkv_short_body(r_ref, lw_ref, km_ref, v_ref, kk_ref, kka_ref, s_ref, *rest):
    y_ref, so_ref = rest[-2:]
    n_seq = s_ref.shape[0]
    grp = 2 * SHORT_T
    row = lax.broadcasted_iota(jnp.int32, (LANES, LANES), 0)
    col = lax.broadcasted_iota(jnp.int32, (LANES, LANES), 1)
    same_blk = (row // SHORT_T) == (col // SHORT_T)
    t_row = row % SHORT_T
    t_col = col % SHORT_T
    strict = same_blk & (t_col < t_row)
    incl = same_blk & (t_col <= t_row)
    eye = (row == col).astype(F32)
    sum_mat = jnp.concatenate([incl.astype(BF16), same_blk.astype(BF16)], axis=0)
    own = ((row // SHORT_T) % 2) == (col // HEAD_DIM)
    zeros_hh = jnp.zeros((HEAD_DIM, HEAD_DIM), F32)

    def rows_of(ref, p):
        return ref[:, :, LANES * p:LANES * (p + 1)].reshape(n_seq * grp, LANES)

    def mask(x):
        return jnp.where(own, x, 0.0)

    pairs = range(N_PAIRS)
    x2, y2, vs, bh, kh, p_last = [], [], [], [], [], []
    for p in pairs:
        lw = rows_of(lw_ref, p)
        l_hi = lw.astype(BF16)
        l_r1 = lw - l_hi.astype(F32)
        l_mid = l_r1.astype(BF16)
        l_lo = (l_r1 - l_mid.astype(F32)).astype(BF16)
        sums = _dg(sum_mat, l_hi) + (_dg(sum_mat, l_mid) + _dg(sum_mat, l_lo))
        cum = sums[:LANES]
        tot = sums[LANES:]
        p_inv = jnp.exp(-cum)
        p_end = jnp.exp(tot - cum)
        kka = rows_of(kka_ref, p)
        km = rows_of(km_ref, p)
        x2.append(jnp.concatenate([mask(-rows_of(kk_ref, p) * jnp.exp(cum - lw)),
                                   mask(rows_of(r_ref, p) * jnp.exp(cum))], axis=0))
        y2.append(jnp.concatenate([mask(kka * p_inv), mask(km * p_inv)], axis=0))
        vs.append(mask(rows_of(v_ref, p)))
        bh.append(mask(kka * p_end))
        kh.append(mask(km * p_end))
        p_last.append(jnp.exp(tot))

    g = [_mm(x2[p], y2[p], _NT, PASS_GRAM) for p in pairs]
    n_mat = [jnp.where(strict, g[p][:LANES, :LANES], 0.0) for p in pairs]
    a_ak = [jnp.where(strict, g[p][:LANES, LANES:], 0.0) for p in pairs]
    a_r = [jnp.concatenate([jnp.where(incl, g[p][LANES:, :LANES], 0.0),
                            jnp.where(incl, g[p][LANES:, LANES:], 0.0)], axis=1) for p in pairs]
    n_sq = [_mm(n_mat[p], n_mat[p], _NN, PASS_INV) for p in pairs]
    t_inv = [(eye + n_mat[p]) + _mm(eye + n_mat[p], n_sq[p], _NN, PASS_INV) for p in pairs]
    akv = [_mm(a_ak[p], vs[p], _NN, PASS_STATE) for p in pairs]

    def seq_rows(x, i):
        return x[grp * i:grp * (i + 1)]

    s_bd, xa_s, xr_s = [], [], []
    for p in pairs:
        s_p, xa_p, xr_p = [], [], []
        for i in range(n_seq):
            top = jnp.concatenate([s_ref[i, 2 * p], zeros_hh], axis=1)
            bot = jnp.concatenate([zeros_hh, s_ref[i, 2 * p + 1]], axis=1)
            s_i = jnp.concatenate([top, bot], axis=0)
            x_i = jnp.concatenate([seq_rows(x2[p][:LANES], i), seq_rows(x2[p][LANES:], i)], axis=0)
            xs_i = _mm(x_i, s_i, _NT, PASS_STATE)
            s_p.append(s_i)
            xa_p.append(xs_i[:grp])
            xr_p.append(xs_i[grp:])
        s_bd.append(s_p)
        xa_s.append(jnp.concatenate(xa_p, axis=0))
        xr_s.append(jnp.concatenate(xr_p, axis=0))

    z_mat = [_mm(t_inv[p], xa_s[p] + akv[p], _NN, PASS_STATE) for p in pairs]
    zv = [jnp.concatenate([z_mat[p], vs[p]], axis=0) for p in pairs]
    y_bd = [xr_s[p] + _mm(a_r[p], zv[p], _NN, PASS_OUT) for p in pairs]
    for p in pairs:
        y3 = y_bd[p].reshape(n_seq, grp, LANES)
        y_ref[:, :, LANES * p:LANES * (p + 1)] = y3 + jnp.concatenate(
            [y3[:, SHORT_T:], y3[:, :SHORT_T]], axis=1)
        for i in range(n_seq):
            zv_i = jnp.concatenate([seq_rows(z_mat[p], i), seq_rows(vs[p], i)], axis=0)
            hk_i = jnp.concatenate([seq_rows(bh[p], i), seq_rows(kh[p], i)], axis=0)
            s_new = (s_bd[p][i] * p_last[p][grp * i:grp * i + 1]
                     + _mm(zv_i, hk_i, _TN, PASS_STATE))
            so_ref[i, 2 * p] = s_new[:HEAD_DIM, :HEAD_DIM]
            so_ref[i, 2 * p + 1] = s_new[HEAD_DIM:, HEAD_DIM:]


def _wkv_short_call(l, r, lw, km, v, kk, kka, s_all, s_new_all):
    n_b = r.shape[0]
    nq = SHORT_SEQS
    seq_block = pl.BlockSpec((nq, 2 * SHORT_T, D_RWKV), lambda i: (i, 0, 0))
    st_block = pl.BlockSpec((None, nq, N_HEADS, HEAD_DIM, HEAD_DIM), lambda i: (l, i, 0, 0, 0))
    carried = [] if s_new_all is None else [s_new_all]
    return pl.pallas_call(
        _wkv_short_body,
        grid=(n_b // nq,),
        in_specs=[seq_block] * 6 + [st_block] + [pl.BlockSpec(memory_space=pl.ANY)] * len(carried),
        out_specs=[seq_block, st_block],
        out_shape=[jax.ShapeDtypeStruct(r.shape, F32), jax.ShapeDtypeStruct(s_all.shape, F32)],
        input_output_aliases={7: 1} if carried else {},
        compiler_params=pltpu.CompilerParams(
            dimension_semantics=("arbitrary",), vmem_limit_bytes=VMEM_LIMIT),
        name="wkv_short",
    )(r, lw, km, v, kk, kka, s_all, *carried)


CONV_ROW_BLOCK = 32
HIST_PAD = 32


def _merge_and_residual(y, bonus, sgr, conv, sgc, x, gate, gnr_g, gnr_b, gnc_g, gnc_b,
                        wout_ref, gpost, bones):
    y_r = (_group_norm(y, gnr_g, gnr_b, RWKV_GN_EPS, bones) + bonus) * sgr
    cn = _group_norm(conv, gnc_g, gnc_b, CONV_GN_EPS, bones)
    y_c = cn * _sigmoid(cn) * sgc
    mix = _dg(y_r.astype(BF16), wout_ref[0:D_RWKV, :]) + _dg(y_c.astype(BF16), wout_ref[D_RWKV:, :])
    ms = jnp.mean(mix * mix, axis=-1, keepdims=True)
    return x + gate * (mix * lax.rsqrt(ms + RMS_EPS) * gpost)


def _post_bm_body(y_ref, bon_ref, sgr_ref, glu_ref, sgc_ref, x_ref, gate_ref, sconv_ref,
                  gnrg_ref, gnrb_ref, wdw_ref, bdw_ref, gncg_ref, gncb_ref, wout_ref, gpost_ref,
                  bones_ref, xo_ref, buf, conv):
    step = pl.program_id(1)
    rows = y_ref.shape[1]
    off = HIST_PAD - HIST
    shifted_rows = rows + HIST_PAD - SUBLANES

    @pl.when(step == 0)
    def _():
        buf[0, 0:off, :] = jnp.zeros((off, D_CONV), F32)
        buf[0, off:HIST_PAD, :] = sconv_ref[0]

    buf[0, HIST_PAD:HIST_PAD + rows, :] = glu_ref[0]
    for q in range(1, SUBLANES):
        buf[q, 0:shifted_rows, :] = buf[0, q:q + shifted_rows, :]
    for r0 in range(0, rows, CONV_ROW_BLOCK):
        acc = jnp.broadcast_to(bdw_ref[...], (CONV_ROW_BLOCK, D_CONV))
        for j in range(CONV_W):
            q = (j + off) % SUBLANES
            base = r0 + j + off - q
            acc = acc + wdw_ref[j:j + 1, :] * buf[q, base:base + CONV_ROW_BLOCK, :]
        conv[r0:r0 + CONV_ROW_BLOCK, :] = acc
    buf[0, off:HIST_PAD, :] = buf[0, rows + off:rows + HIST_PAD, :]

    xo_ref[0] = _merge_and_residual(
        y_ref[0], bon_ref[0], sgr_ref[0], conv[...], sgc_ref[0], x_ref[0], gate_ref[...],
        gnrg_ref[...], gnrb_ref[...], gncg_ref[...], gncb_ref[...], wout_ref, gpost_ref[...],
        bones_ref[...])


def _post_tm_body(y_ref, bon_ref, sgr_ref, glu_ref, sgc_ref, x_ref, gate_ref, sconv_ref,
                  gnrg_ref, gnrb_ref, wdw_ref, bdw_ref, gncg_ref, gncb_ref, wout_ref, gpost_ref,
                  bones_ref, xo_ref):
    n_t, rows, _ = y_ref.shape
    for t in range(n_t):
        acc = jnp.broadcast_to(bdw_ref[...], (rows, D_CONV))
        for j in range(CONV_W):
            i = t + j
            src = sconv_ref[i] if i < HIST else glu_ref[i - HIST]
            acc = acc + wdw_ref[j:j + 1, :] * src
        xo_ref[t] = _merge_and_residual(
            y_ref[t], bon_ref[t], sgr_ref[t], acc, sgc_ref[t], x_ref[t], gate_ref[...],
            gnrg_ref[...], gnrb_ref[...], gncg_ref[...], gncb_ref[...], wout_ref, gpost_ref[...],
            bones_ref[...])


def _post_call(time_major, l, y, bonus, sgr, glu, sgc, x, mod, s_conv, layer_params, bones, tile):
    n_outer, n_inner, _ = x.shape
    if time_major:
        rows = n_inner
        grid = (1, 1)
        lead = n_outer
        x_map = lambda i, t: (0, 0, 0)
        gate_spec = pl.BlockSpec((None, rows, D_MODEL), lambda i, t: (l, 0, 2))
        sconv_spec = _layer_spec(s_conv, l, 2)
        scratch = []
        body = _post_tm_body
    else:
        rows = tile
        grid = (n_outer, n_inner // rows)
        lead = 1
        x_map = lambda b, t: (b, t, 0)
        gate_spec = pl.BlockSpec((None, None, 1, D_MODEL), lambda b, t: (l, b, 0, 2))
        sconv_spec = pl.BlockSpec((1, HIST, D_CONV), lambda b, t: (b, 0, 0))
        scratch = [pltpu.VMEM((SUBLANES, HIST_PAD + rows, D_CONV), F32),
                   pltpu.VMEM((rows, D_CONV), F32)]
        body = _post_bm_body
    half = pl.BlockSpec((lead, rows, D_RWKV), x_map)
    full = pl.BlockSpec((lead, rows, D_MODEL), x_map)
    return pl.pallas_call(
        body,
        grid=grid,
        in_specs=[half] * 5 + [full, gate_spec, sconv_spec]
        + [_layer_spec(p, l, 2) for p in layer_params]
        + [pl.BlockSpec(bones.shape, lambda i, t: (0, 0))],
        out_specs=full,
        out_shape=jax.ShapeDtypeStruct(x.shape, F32),
        scratch_shapes=scratch,
        compiler_params=pltpu.CompilerParams(
            dimension_semantics=("arbitrary", "arbitrary"), vmem_limit_bytes=VMEM_LIMIT),
        name="post_tm" if time_major else "post_bm",
    )(y, bonus, sgr, glu, sgc, x, mod, s_conv, *layer_params, bones)


PROMPT_TILE = 256


def _block_ones():
    i = lax.broadcasted_iota(jnp.int32, (MXU_DIM, MXU_DIM), 0) // HEAD_DIM
    j = lax.broadcasted_iota(jnp.int32, (MXU_DIM, MXU_DIM), 1) // HEAD_DIM
    return (i == j).astype(BF16)


def _lower_tri():
    i = lax.broadcasted_iota(jnp.int32, (CHUNK, CHUNK), 0)
    j = lax.broadcasted_iota(jnp.int32, (CHUNK, CHUNK), 1)
    return (j <= i).astype(BF16)


def kernel(x_prompt, x_sample, c_prompt, c_sample, state_shift, state_wkv, state_conv, w_ada, b_ada, g_pre, g_post, w_in, mu, w0, w_up, a0, a_up, k_k, k_a, r_k, gn_r_g, gn_r_b, w_dw, b_dw, gn_c_g, gn_c_b, w_out):
    depth = w_in.shape[0]
    n_p, t_p, _ = x_prompt.shape
    n_s, t_s, _ = x_sample.shape
    bones = _block_ones()
    lt = _lower_tri()

    mod_p, mod_s = _mod_call(c_prompt, c_sample, w_ada, b_ada)
    mod_p = mod_p[:, :, None, :]

    xp = x_prompt
    xs = jnp.transpose(x_sample, (1, 0, 2))
    zeros_shift = jnp.zeros((n_p, 1, D_MODEL), F32)
    zeros_wkv = jnp.zeros((n_p, N_HEADS, HEAD_DIM, HEAD_DIM), F32)
    zeros_conv = jnp.zeros((n_p, HIST, D_CONV), F32)
    sconv_tm = jnp.transpose(state_conv, (0, 2, 1, 3))

    def dup_rows(a):
        a = jnp.transpose(a, (1, 0, 2))
        return jnp.concatenate([a, a], axis=1)

    rows = lambda a: a.reshape(depth, 1, -1)
    zl = jnp.zeros((depth, LORA, D_RWKV), F32)
    w_lora = jnp.concatenate(
        [jnp.concatenate([w_up, zl], axis=2), jnp.concatenate([zl, a_up], axis=2)],
        axis=1).astype(BF16)
    in_params = (rows(g_pre), w_in.astype(BF16), rows(mu), rows(w0), rows(a0), w_lora,
                 rows(k_k), rows(k_a), rows(r_k))
    post_params = (rows(gn_r_g), rows(gn_r_b), w_dw, rows(b_dw), rows(gn_c_g), rows(gn_c_b),
                   w_out.astype(BF16), rows(g_post))

    shifts_p, wkvs_p, convs_p, shifts_s, convs_s = [], [], [], [], []
    wkv_s = None
    for l in range(depth):
        (r, lw, km, v, kk, kka, bonus, sgr, glu, sgc, hl) = _inproj_call(
            False, l, xp, mod_p, zeros_shift, in_params, bones, PROMPT_TILE)
        y, s_new = _wkv_chunk_call(r, lw, km, v, kk, kka, zeros_wkv, lt)
        xp = _post_call(False, l, y, bonus, sgr, glu, sgc, xp, mod_p, zeros_conv, post_params,
                        bones, PROMPT_TILE)
        shifts_p.append(hl[:, 0])
        wkvs_p.append(s_new)
        convs_p.append(glu[:, -HIST:] if t_p >= HIST
                       else jnp.concatenate([zeros_conv, glu], axis=1)[:, -HIST:])

        (r, lw, km, v, kk, kka, bonus, sgr, glu, sgc, hl) = _inproj_call(
            True, l, xs, mod_s, state_shift, in_params, bones, None)
        y, wkv_s = _wkv_short_call(l, *(dup_rows(a) for a in (r, lw, km, v, kk, kka)),
                                   state_wkv, wkv_s)
        y = jnp.transpose(y[:, :SHORT_T], (1, 0, 2))
        xs = _post_call(True, l, y, bonus, sgr, glu, sgc, xs, mod_s, sconv_tm, post_params,
                        bones, None)
        shifts_s.append(hl)
        convs_s.append(jnp.concatenate(
            [state_conv[l][:, t_s:], jnp.transpose(glu, (1, 0, 2))], axis=1)[:, -HIST:])

    return (xp, jnp.transpose(xs, (1, 0, 2)),
            jnp.stack(shifts_p), jnp.stack(wkvs_p), jnp.stack(convs_p),
            jnp.stack(shifts_s), wkv_s, jnp.stack(convs_s))
```

```python
import functools

import jax
import jax.numpy as jnp
from jax import lax
from jax.experimental import pallas as pl
from jax.experimental.pallas import tpu as pltpu

F32 = jnp.float32
BF16 = jnp.bfloat16

D_MODEL = 1024
D_RWKV = 512
D_CONV = 512
HEAD_DIM = 64
N_HEADS = 8
N_PAIRS = N_HEADS // 2
LORA = 64
P_RWKV = 4 * D_RWKV + 2 * LORA
P_IN = P_RWKV + 3 * D_CONV
CONV_W = 31
HIST = CONV_W - 1
RMS_EPS = 1e-6
RWKV_GN_EPS = 64e-5
CONV_GN_EPS = 1e-5

LANES = 128
SUBLANES = 8
MXU_DIM = 256
CHUNK = 64
VMEM_LIMIT = 56 * 1024 * 1024

_NN = (((1,), (0,)), ((), ()))
_NT = (((1,), (1,)), ((), ()))
_TN = (((0,), (0,)), ((), ()))


def _dg(a, b, dims=_NN):
    return lax.dot_general(a, b, dims, preferred_element_type=F32)


def _split2(x):
    hi = x.astype(BF16)
    lo = (x - hi.astype(F32)).astype(BF16)
    return hi, lo


def _mm(a, b, dims=_NN, passes=1):
    if passes == 1:
        return _dg(a.astype(BF16), b.astype(BF16), dims)
    ah, al = _split2(a)
    bh, bl = _split2(b)
    return _dg(ah, bh, dims) + (_dg(ah, bl, dims) + _dg(al, bh, dims))


def _segsum(x, bones):
    hi, lo = _split2(x)
    cols = [slice(c0, c0 + MXU_DIM) for c0 in range(0, x.shape[1], MXU_DIM)]
    return jnp.concatenate([_dg(hi[:, s], bones) + _dg(lo[:, s], bones) for s in cols], axis=1)


def _sigmoid(x):
    return 1.0 / (1.0 + jnp.exp(-x))


def _group_norm(x, g, b, eps, bones):
    inv_n = 1.0 / HEAD_DIM
    mean = _segsum(x, bones) * inv_n
    d = x - mean
    var = _segsum(d * d, bones) * inv_n
    return d * lax.rsqrt(var + eps) * g + b


def _mod_body(cp_ref, cs_ref, w_ref, b_ref, op_ref, os_ref):
    w = w_ref[...].astype(BF16)
    for c_ref, o_ref in ((cp_ref, op_ref), (cs_ref, os_ref)):
        c = c_ref[...]
        o_ref[...] = _dg((c * _sigmoid(c)).astype(BF16), w) + b_ref[...]


def _mod_call(c_prompt, c_sample, w_ada, b_ada):
    depth, _, n3 = w_ada.shape
    bn = 768
    layer_cols = lambda l, j: (l, 0, j)
    whole = lambda a: pl.BlockSpec(a.shape, lambda l, j: (0, 0))
    out = lambda a: pl.BlockSpec((None, a.shape[0], bn), layer_cols)
    return pl.pallas_call(
        _mod_body,
        grid=(depth, n3 // bn),
        in_specs=[whole(c_prompt), whole(c_sample),
                  pl.BlockSpec((None, D_MODEL, bn), layer_cols),
                  pl.BlockSpec((None, 1, bn), layer_cols)],
        out_specs=[out(c_prompt), out(c_sample)],
        out_shape=[jax.ShapeDtypeStruct((depth, c.shape[0], n3), F32) for c in (c_prompt, c_sample)],
        compiler_params=pltpu.CompilerParams(
            dimension_semantics=("arbitrary", "arbitrary"), vmem_limit_bytes=VMEM_LIMIT),
        name="adaln_mod",
    )(c_prompt, c_sample, w_ada, b_ada.reshape(depth, 1, n3))


def _layer_spec(a, l, n_grid):
    idx = (l,) + (0,) * (a.ndim - 1)
    index_map = (lambda i: idx) if n_grid == 1 else (lambda i, t: idx)
    return pl.BlockSpec((None,) + a.shape[1:], index_map, pipeline_mode=pl.Buffered(1))


INPROJ_SUB = 256
INPROJ_TILE = 512


def _inproj_body(time_major, x_ref, sh_ref, sc_ref, ss_ref, gpre_ref, win_ref, mu_ref,
                 w0_ref, a0_ref, wl_ref, kkp_ref, kap_ref, rkp_ref, bones_ref,
                 r_o, lw_o, km_o, v_o, kk_o, kka_o, bon_o, sgr_o, glu_o, sgc_o, hl_o, carry):
    step = pl.program_id(1)
    rows = x_ref.shape[1]
    sub = min(rows, INPROJ_SUB)
    bounds = [(r0, r0 + sub) for r0 in range(0, rows, sub)]

    @pl.when(step == 0)
    def _():
        if time_major:
            carry[...] = _dg(ss_ref[...].astype(BF16), win_ref[:, :P_RWKV])
        else:
            ss8 = jnp.broadcast_to(ss_ref[...], (SUBLANES, D_MODEL)).astype(BF16)
            carry[SUBLANES - 1:SUBLANES, :] = _dg(ss8, win_ref[:, :P_RWKV])[0:1]

    def cols(hb, c0, c1):
        return _dg(hb, win_ref[:, c0:c1])

    def shifted(ur, r0, r1):
        if time_major:
            prev = carry[r0:r1, :]
            carry[r0:r1, :] = ur
        else:
            carry[SUBLANES + r0:SUBLANES + r1, :] = ur
            prev = carry[SUBLANES - 1 + r0:SUBLANES - 1 + r1, :]
        return prev

    outs = (r_o, lw_o, km_o, v_o, kk_o, kka_o, bon_o, sgr_o, glu_o, sgc_o)
    params = (mu_ref, w0_ref, a0_ref, wl_ref, kkp_ref, kap_ref, rkp_ref, bones_ref)
    half = 2 * D_RWKV
    pending = iter(())
    for r0, r1 in bounds:
        x = x_ref[0, r0:r1]
        ms = jnp.mean(x * x, axis=-1, keepdims=True)
        h = x * lax.rsqrt(ms + RMS_EPS) * gpre_ref[...]
        if time_major:
            h = h * (1.0 + sc_ref[r0:r1]) + sh_ref[r0:r1]
            hl_o[r0:r1] = h
        else:
            h = h * (1.0 + sc_ref[...]) + sh_ref[...]
            if r1 == rows:
                hl_o[...] = h[sub - 1:sub]
        hb = h.astype(BF16)
        ur_a = cols(hb, 0, half)
        next(pending, None)
        ur = jnp.concatenate([ur_a, cols(hb, half, P_RWKV)], axis=1)
        next(pending, None)
        conv_blocks = [cols(hb, P_RWKV, P_RWKV + D_CONV)]
        next(pending, None)
        conv_blocks += [cols(hb, P_RWKV + j * D_CONV, P_RWKV + (j + 1) * D_CONV) for j in (1, 2)]
        for _ in pending:
            pass
        prev = shifted(ur, r0, r1)
        pending = _inproj_rows(ur, prev, conv_blocks, slice(r0, r1), params, outs)
    for _ in pending:
        pass
    if not time_major:
        carry[SUBLANES - 1:SUBLANES, :] = ur[sub - 1:sub]


def _inproj_rows(ur, prev, conv_blocks, rs, params, outs):
    mu_ref, w0_ref, a0_ref, wl_ref, kkp_ref, kap_ref, rkp_ref, bones_ref = params
    r_o, lw_o, km_o, v_o, kk_o, kka_o, bon_o, sgr_o, glu_o, sgc_o = outs
    glu_a, glu_b, g_c = conv_blocks
    urs = ur + (prev - ur) * mu_ref[...]
    r = urs[:, 0:D_RWKV]
    k = urs[:, D_RWKV:2 * D_RWKV]
    v = urs[:, 2 * D_RWKV:3 * D_RWKV]
    g_r = urs[:, 3 * D_RWKV:4 * D_RWKV]
    lora = urs[:, 4 * D_RWKV:P_RWKV]
    lane = lax.broadcasted_iota(jnp.int32, lora.shape, 1)
    lora_in = jnp.where(lane < LORA, jnp.tanh(lora), lora).astype(BF16)
    wa = _dg(lora_in, wl_ref[...])
    r_o[0, rs] = r
    v_o[0, rs] = v
    glu_o[0, rs] = glu_a * _sigmoid(glu_b)
    sgc_o[0, rs] = g_c * _sigmoid(g_c)
    yield

    z = -(w0_ref[...] + wa[:, :D_RWKV])
    softplus = jnp.maximum(z, 0.0) + jnp.log(1.0 + jnp.exp(-jnp.abs(z)))
    lw = -jnp.exp(-softplus - 0.5)
    a = _sigmoid(a0_ref[...] + wa[:, D_RWKV:])
    lw_o[0, rs] = lw
    bones = bones_ref[...]
    kkr = k * kkp_ref[...]
    ssq = _segsum(kkr * kkr, bones)
    yield

    kk = kkr * lax.rsqrt(jnp.maximum(ssq, 1e-24))
    kk_o[0, rs] = kk
    kka_o[0, rs] = kk * a
    km = k * (1.0 + (a - 1.0) * kap_ref[...])
    km_o[0, rs] = km
    rkk = _segsum(r * km * rkp_ref[...], bones)
    yield

    bon_o[0, rs] = rkk * v
    sgr_o[0, rs] = g_r * _sigmoid(g_r)


def _inproj_call(time_major, l, x, mod, s_shift, layer_params, bones, tile):
    n_outer, n_inner, _ = x.shape
    if time_major:
        grid = (1, n_outer)
        rows = n_inner
        x_map = lambda i, t: (t, 0, 0)
        mod_spec = lambda j: pl.BlockSpec((None, rows, D_MODEL), lambda i, t: (l, 0, j))
        shift_spec = _layer_spec(s_shift, l, 2)
        carry = pltpu.VMEM((rows, P_RWKV), F32)
        hl_shape = (rows, D_MODEL)
        hl_spec = pl.BlockSpec(hl_shape, lambda i, t: (0, 0))
    else:
        rows = tile
        grid = (n_outer, n_inner // rows)
        x_map = lambda b, t: (b, t, 0)
        mod_spec = lambda j: pl.BlockSpec((None, None, 1, D_MODEL), lambda b, t: (l, b, 0, j))
        shift_spec = pl.BlockSpec((None, 1, D_MODEL), lambda b, t: (b, 0, 0))
        carry = pltpu.VMEM((rows + SUBLANES, P_RWKV), F32)
        hl_shape = (n_outer, 1, D_MODEL)
        hl_spec = pl.BlockSpec((None, 1, D_MODEL), lambda b, t: (b, 0, 0))
    out_block = pl.BlockSpec((1, rows, D_RWKV), x_map)
    out_sds = jax.ShapeDtypeStruct((n_outer, n_inner, D_RWKV), F32)
    return pl.pallas_call(
        functools.partial(_inproj_body, time_major),
        grid=grid,
        in_specs=[pl.BlockSpec((1, rows, D_MODEL), x_map), mod_spec(0), mod_spec(1), shift_spec]
        + [_layer_spec(p, l, 2) for p in layer_params]
        + [pl.BlockSpec(bones.shape, lambda i, t: (0, 0))],
        out_specs=[out_block] * 10 + [hl_spec],
        out_shape=[out_sds] * 10 + [jax.ShapeDtypeStruct(hl_shape, F32)],
        scratch_shapes=[carry],
        compiler_params=pltpu.CompilerParams(
            dimension_semantics=("arbitrary", "arbitrary"), vmem_limit_bytes=VMEM_LIMIT),
        name="inproj_tm" if time_major else "inproj_bm",
    )(x, mod, mod, s_shift, *layer_params, bones)


PASS_GRAM = 1
PASS_INV = 1
PASS_STATE = 1
PASS_OUT = 1
CHUNK_SEQS = 8


def _wkv_chunk_body(r_ref, lw_ref, km_ref, v_ref, kk_ref, kka_ref, s0_ref, lt_ref,
                    y_ref, so_ref, s_scr):
    c = pl.program_id(1)
    n_seq = r_ref.shape[0]
    zeros_hh = jnp.zeros((HEAD_DIM, HEAD_DIM), F32)

    @pl.when(c == 0)
    def _():
        for i in range(n_seq):
            for p in range(N_PAIRS):
                top = jnp.concatenate([s0_ref[i, 2 * p], zeros_hh], axis=1)
                bot = jnp.concatenate([zeros_hh, s0_ref[i, 2 * p + 1]], axis=1)
                s_scr[i * N_PAIRS + p] = jnp.concatenate([top, bot], axis=0)

    lt = lt_ref[...]
    lane = lax.broadcasted_iota(jnp.int32, (CHUNK, LANES), 1)
    first = lane < HEAD_DIM
    row = lax.broadcasted_iota(jnp.int32, (LANES, LANES), 0)
    col = lax.broadcasted_iota(jnp.int32, (LANES, LANES), 1)
    same_head = (row < HEAD_DIM) == (col < HEAD_DIM)
    rt_i = row & (HEAD_DIM - 1)
    ct_i = col & (HEAD_DIM - 1)
    tri_strict = same_head & (ct_i < rt_i)
    tri_incl = same_head & (ct_i <= rt_i)
    eye = (row == col).astype(F32)

    def stack(xp):
        return jnp.concatenate([jnp.where(first, xp, 0.0), jnp.where(first, 0.0, xp)], axis=0)

    x2, y2, vs, hk, p_last = [], [], [], [], []
    for i in range(n_seq):
        lw = lw_ref[i]
        l_hi = lw.astype(BF16)
        l_r1 = lw - l_hi.astype(F32)
        l_mid = l_r1.astype(BF16)
        l_lo = (l_r1 - l_mid.astype(F32)).astype(BF16)
        cum = _dg(lt, l_hi) + (_dg(lt, l_mid) + _dg(lt, l_lo))
        cum_last = cum[CHUNK - 1:CHUNK, :]
        p_inv = jnp.exp(-cum)
        p_end = jnp.exp(cum_last - cum)
        at_all = -kk_ref[i] * jnp.exp(cum - lw)
        rt_all = r_ref[i] * jnp.exp(cum)
        kka = kka_ref[i]
        km = km_ref[i]
        bt_all = kka * p_inv
        kt_all = km * p_inv
        bh_all = kka * p_end
        kh_all = km * p_end
        v_all = v_ref[i]
        pl_all = jnp.exp(cum_last)
        for p in range(N_PAIRS):
            s = slice(LANES * p, LANES * (p + 1))
            x2.append(jnp.concatenate([stack(at_all[:, s]), stack(rt_all[:, s])], axis=0))
            y2.append(jnp.concatenate([stack(bt_all[:, s]), stack(kt_all[:, s])], axis=0))
            vs.append(stack(v_all[:, s]))
            hk.append(jnp.concatenate([stack(bh_all[:, s]), stack(kh_all[:, s])], axis=0))
            p_last.append(pl_all[:, s])

    units = range(n_seq * N_PAIRS)
    g = [_mm(x2[u], y2[u], _NT, PASS_GRAM) for u in units]
    n_pow = [jnp.where(tri_strict, g[u][:LANES, :LANES], 0.0) for u in units]
    a_ak = [jnp.where(tri_strict, g[u][:LANES, LANES:], 0.0) for u in units]
    a_r = [jnp.concatenate([jnp.where(tri_incl, g[u][LANES:, :LANES], 0.0),
                            jnp.where(tri_incl, g[u][LANES:, LANES:], 0.0)], axis=1) for u in units]

    t_inv = [eye + n_pow[u] for u in units]
    span = 2
    while span < CHUNK:
        n_pow = [_mm(n_pow[u], n_pow[u], _NN, PASS_INV) for u in units]
        t_inv = [t_inv[u] + _mm(t_inv[u], n_pow[u], _NN, PASS_INV) for u in units]
        span *= 2

    s_bd = [s_scr[u] for u in units]
    xs = [_mm(x2[u], s_bd[u], _NT, PASS_STATE) for u in units]
    akv = [_mm(a_ak[u], vs[u], _NN, PASS_STATE) for u in units]
    z_mat = [_mm(t_inv[u], xs[u][:LANES] + akv[u], _NN, PASS_STATE) for u in units]
    zv = [jnp.concatenate([z_mat[u], vs[u]], axis=0) for u in units]
    y_bd = [xs[u][LANES:] + _mm(a_r[u], zv[u], _NN, PASS_OUT) for u in units]
    s_new = [s_bd[u] * p_last[u] + _mm(zv[u], hk[u], _TN, PASS_STATE) for u in units]
    for u in units:
        i, p = divmod(u, N_PAIRS)
        y_ref[i, :, LANES * p:LANES * (p + 1)] = y_bd[u][:HEAD_DIM] + y_bd[u][HEAD_DIM:]
        s_scr[u] = s_new[u]
        so_ref[i, 2 * p] = s_new[u][:HEAD_DIM, :HEAD_DIM]
        so_ref[i, 2 * p + 1] = s_new[u][HEAD_DIM:, HEAD_DIM:]


def _wkv_chunk_call(r, lw, km, v, kk, kka, s0, lt):
    n_b, n_t, _ = r.shape
    nq = CHUNK_SEQS
    seq_block = pl.BlockSpec((nq, CHUNK, D_RWKV), lambda b, c: (b, c, 0))
    st_block = pl.BlockSpec((nq, N_HEADS, HEAD_DIM, HEAD_DIM), lambda b, c: (b, 0, 0, 0))
    return pl.pallas_call(
        _wkv_chunk_body,
        grid=(n_b // nq, n_t // CHUNK),
        in_specs=[seq_block] * 6 + [st_block, pl.BlockSpec(lt.shape, lambda b, c: (0, 0))],
        out_specs=[seq_block, st_block],
        out_shape=[jax.ShapeDtypeStruct(r.shape, F32), jax.ShapeDtypeStruct(s0.shape, F32)],
        scratch_shapes=[pltpu.VMEM((nq * N_PAIRS, LANES, LANES), F32)],
        compiler_params=pltpu.CompilerParams(
            dimension_semantics=("arbitrary", "arbitrary"), vmem_limit_bytes=VMEM_LIMIT),
        name="wkv_chunk",
    )(r, lw, km, v, kk, kka, s0, lt)


SHORT_T = 4
SHORT_SEQS = LANES // (2 * SHORT_T)


def _wkv_short_body(r_ref, lw_ref, km_ref, v_ref, kk_ref, kka_ref, s_ref, *rest):
    y_ref, so_ref = rest[-2:]
    n_seq = s_ref.shape[0]
    grp = 2 * SHORT_T
    row = lax.broadcasted_iota(jnp.int32, (LANES, LANES), 0)
    col = lax.broadcasted_iota(jnp.int32, (LANES, LANES), 1)
    same_blk = (row // SHORT_T) == (col // SHORT_T)
    t_row = row % SHORT_T
    t_col = col % SHORT_T
    strict = same_blk & (t_col < t_row)
    incl = same_blk & (t_col <= t_row)
    eye = (row == col).astype(F32)
    sum_mat = jnp.concatenate([incl.astype(BF16), same_blk.astype(BF16)], axis=0)
    own = ((row // SHORT_T) % 2) == (col // HEAD_DIM)
    zeros_hh = jnp.zeros((HEAD_DIM, HEAD_DIM), F32)

    def rows_of(ref, p):
        return ref[:, :, LANES * p:LANES * (p + 1)].reshape(n_seq * grp, LANES)

    def mask(x):
        return jnp.where(own, x, 0.0)

    pairs = range(N_PAIRS)
    x2, y2, vs, bh, kh, p_last = [], [], [], [], [], []
    for p in pairs:
        lw = rows_of(lw_ref, p)
        l_hi = lw.astype(BF16)
        l_r1 = lw - l_hi.astype(F32)
        l_mid = l_r1.astype(BF16)
        l_lo = (l_r1 - l_mid.astype(F32)).astype(BF16)
        sums = _dg(sum_mat, l_hi) + (_dg(sum_mat, l_mid) + _dg(sum_mat, l_lo))
        cum = sums[:LANES]
        tot = sums[LANES:]
        p_inv = jnp.exp(-cum)
        p_end = jnp.exp(tot - cum)
        kka = rows_of(kka_ref, p)
        km = rows_of(km_ref, p)
        x2.append(jnp.concatenate([mask(-rows_of(kk_ref, p) * jnp.exp(cum - lw)),
                                   mask(rows_of(r_ref, p) * jnp.exp(cum))], axis=0))
        y2.append(jnp.concatenate([mask(kka * p_inv), mask(km * p_inv)], axis=0))
        vs.append(mask(rows_of(v_ref, p)))
        bh.append(mask(kka * p_end))
        kh.append(mask(km * p_end))
        p_last.append(jnp.exp(tot))

    g = [_mm(x2[p], y2[p], _NT, PASS_GRAM) for p in pairs]
    n_mat = [jnp.where(strict, g[p][:LANES, :LANES], 0.0) for p in pairs]
    a_ak = [jnp.where(strict, g[p][:LANES, LANES:], 0.0) for p in pairs]
    a_r = [jnp.concatenate([jnp.where(incl, g[p][LANES:, :LANES], 0.0),
                            jnp.where(incl, g[p][LANES:, LANES:], 0.0)], axis=1) for p in pairs]
    n_sq = [_mm(n_mat[p], n_mat[p], _NN, PASS_INV) for p in pairs]
    t_inv = [(eye + n_mat[p]) + _mm(eye + n_mat[p], n_sq[p], _NN, PASS_INV) for p in pairs]
    akv = [_mm(a_ak[p], vs[p], _NN, PASS_STATE) for p in pairs]

    def seq_rows(x, i):
        return x[grp * i:grp * (i + 1)]

    s_bd, xa_s, xr_s = [], [], []
    for p in pairs:
        s_p, xa_p, xr_p = [], [], []
        for i in range(n_seq):
            top = jnp.concatenate([s_ref[i, 2 * p], zeros_hh], axis=1)
            bot = jnp.concatenate([zeros_hh, s_ref[i, 2 * p + 1]], axis=1)
            s_i = jnp.concatenate([top, bot], axis=0)
            x_i = jnp.concatenate([seq_rows(x2[p][:LANES], i), seq_rows(x2[p][LANES:], i)], axis=0)
            xs_i = _mm(x_i, s_i, _NT, PASS_STATE)
            s_p.append(s_i)
            xa_p.append(xs_i[:grp])
            xr_p.append(xs_i[grp:])
        s_bd.append(s_p)
        xa_s.append(jnp.concatenate(xa_p, axis=0))
        xr_s.append(jnp.concatenate(xr_p, axis=0))

    z_mat = [_mm(t_inv[p], xa_s[p] + akv[p], _NN, PASS_STATE) for p in pairs]
    zv = [jnp.concatenate([z_mat[p], vs[p]], axis=0) for p in pairs]
    y_bd = [xr_s[p] + _mm(a_r[p], zv[p], _NN, PASS_OUT) for p in pairs]
    for p in pairs:
        y3 = y_bd[p].reshape(n_seq, grp, LANES)
        y_ref[:, :, LANES * p:LANES * (p + 1)] = y3 + jnp.concatenate(
            [y3[:, SHORT_T:], y3[:, :SHORT_T]], axis=1)
        for i in range(n_seq):
            zv_i = jnp.concatenate([seq_rows(z_mat[p], i), seq_rows(vs[p], i)], axis=0)
            hk_i = jnp.concatenate([seq_rows(bh[p], i), seq_rows(kh[p], i)], axis=0)
            s_new = (s_bd[p][i] * p_last[p][grp * i:grp * i + 1]
                     + _mm(zv_i, hk_i, _TN, PASS_STATE))
            so_ref[i, 2 * p] = s_new[:HEAD_DIM, :HEAD_DIM]
            so_ref[i, 2 * p + 1] = s_new[HEAD_DIM:, HEAD_DIM:]


def _wkv_short_call(l, r, lw, km, v, kk, kka, s_all, s_new_all):
    n_b = r.shape[0]
    nq = SHORT_SEQS
    seq_block = pl.BlockSpec((nq, 2 * SHORT_T, D_RWKV), lambda i: (i, 0, 0))
    st_block = pl.BlockSpec((None, nq, N_HEADS, HEAD_DIM, HEAD_DIM), lambda i: (l, i, 0, 0, 0))
    carried = [] if s_new_all is None else [s_new_all]
    return pl.pallas_call(
        _wkv_short_body,
        grid=(n_b // nq,),
        in_specs=[seq_block] * 6 + [st_block] + [pl.BlockSpec(memory_space=pl.ANY)] * len(carried),
        out_specs=[seq_block, st_block],
        out_shape=[jax.ShapeDtypeStruct(r.shape, F32), jax.ShapeDtypeStruct(s_all.shape, F32)],
        input_output_aliases={7: 1} if carried else {},
        compiler_params=pltpu.CompilerParams(
            dimension_semantics=("arbitrary",), vmem_limit_bytes=VMEM_LIMIT),
        name="wkv_short",
    )(r, lw, km, v, kk, kka, s_all, *carried)


CONV_ROW_BLOCK = 32
HIST_PAD = 32


def _merge_and_residual(y, bonus, sgr, conv, sgc, x, gate, gnr_g, gnr_b, gnc_g, gnc_b,
                        wout_ref, gpost, bones):
    y_r = (_group_norm(y, gnr_g, gnr_b, RWKV_GN_EPS, bones) + bonus) * sgr
    cn = _group_norm(conv, gnc_g, gnc_b, CONV_GN_EPS, bones)
    y_c = cn * _sigmoid(cn) * sgc
    mix = _dg(y_r.astype(BF16), wout_ref[0:D_RWKV, :]) + _dg(y_c.astype(BF16), wout_ref[D_RWKV:, :])
    ms = jnp.mean(mix * mix, axis=-1, keepdims=True)
    return x + gate * (mix * lax.rsqrt(ms + RMS_EPS) * gpost)


def _post_bm_body(y_ref, bon_ref, sgr_ref, glu_ref, sgc_ref, x_ref, gate_ref, sconv_ref,
                  gnrg_ref, gnrb_ref, wdw_ref, bdw_ref, gncg_ref, gncb_ref, wout_ref, gpost_ref,
                  bones_ref, xo_ref, buf, conv):
    step = pl.program_id(1)
    rows = y_ref.shape[1]
    off = HIST_PAD - HIST
    shifted_rows = rows + HIST_PAD - SUBLANES

    @pl.when(step == 0)
    def _():
        buf[0, 0:off, :] = jnp.zeros((off, D_CONV), F32)
        buf[0, off:HIST_PAD, :] = sconv_ref[0]

    buf[0, HIST_PAD:HIST_PAD + rows, :] = glu_ref[0]
    for q in range(1, SUBLANES):
        buf[q, 0:shifted_rows, :] = buf[0, q:q + shifted_rows, :]
    for r0 in range(0, rows, CONV_ROW_BLOCK):
        acc = jnp.broadcast_to(bdw_ref[...], (CONV_ROW_BLOCK, D_CONV))
        for j in range(CONV_W):
            q = (j + off) % SUBLANES
            base = r0 + j + off - q
            acc = acc + wdw_ref[j:j + 1, :] * buf[q, base:base + CONV_ROW_BLOCK, :]
        conv[r0:r0 + CONV_ROW_BLOCK, :] = acc
    buf[0, off:HIST_PAD, :] = buf[0, rows + off:rows + HIST_PAD, :]

    xo_ref[0] = _merge_and_residual(
        y_ref[0], bon_ref[0], sgr_ref[0], conv[...], sgc_ref[0], x_ref[0], gate_ref[...],
        gnrg_ref[...], gnrb_ref[...], gncg_ref[...], gncb_ref[...], wout_ref, gpost_ref[...],
        bones_ref[...])


def _post_tm_body(y_ref, bon_ref, sgr_ref, glu_ref, sgc_ref, x_ref, gate_ref, sconv_ref,
                  gnrg_ref, gnrb_ref, wdw_ref, bdw_ref, gncg_ref, gncb_ref, wout_ref, gpost_ref,
                  bones_ref, xo_ref):
    n_t, rows, _ = y_ref.shape
    for t in range(n_t):
        acc = jnp.broadcast_to(bdw_ref[...], (rows, D_CONV))
        for j in range(CONV_W):
            i = t + j
            src = sconv_ref[i] if i < HIST else glu_ref[i - HIST]
            acc = acc + wdw_ref[j:j + 1, :] * src
        xo_ref[t] = _merge_and_residual(
            y_ref[t], bon_ref[t], sgr_ref[t], acc, sgc_ref[t], x_ref[t], gate_ref[...],
            gnrg_ref[...], gnrb_ref[...], gncg_ref[...], gncb_ref[...], wout_ref, gpost_ref[...],
            bones_ref[...])


def _post_call(time_major, l, y, bonus, sgr, glu, sgc, x, mod, s_conv, layer_params, bones, tile):
    n_outer, n_inner, _ = x.shape
    if time_major:
        rows = n_inner
        grid = (1, 1)
        lead = n_outer
        x_map = lambda i, t: (0, 0, 0)
        gate_spec = pl.BlockSpec((None, rows, D_MODEL), lambda i, t: (l, 0, 2))
        sconv_spec = _layer_spec(s_conv, l, 2)
        scratch = []
        body = _post_tm_body
    else:
        rows = tile
        grid = (n_outer, n_inner // rows)
        lead = 1
        x_map = lambda b, t: (b, t, 0)
        gate_spec = pl.BlockSpec((None, None, 1, D_MODEL), lambda b, t: (l, b, 0, 2))
        sconv_spec = pl.BlockSpec((1, HIST, D_CONV), lambda b, t: (b, 0, 0))
        scratch = [pltpu.VMEM((SUBLANES, HIST_PAD + rows, D_CONV), F32),
                   pltpu.VMEM((rows, D_CONV), F32)]
        body = _post_bm_body
    half = pl.BlockSpec((lead, rows, D_RWKV), x_map)
    full = pl.BlockSpec((lead, rows, D_MODEL), x_map)
    return pl.pallas_call(
        body,
        grid=grid,
        in_specs=[half] * 5 + [full, gate_spec, sconv_spec]
        + [_layer_spec(p, l, 2) for p in layer_params]
        + [pl.BlockSpec(bones.shape, lambda i, t: (0, 0))],
        out_specs=full,
        out_shape=jax.ShapeDtypeStruct(x.shape, F32),
        scratch_shapes=scratch,
        compiler_params=pltpu.CompilerParams(
            dimension_semantics=("arbitrary", "arbitrary"), vmem_limit_bytes=VMEM_LIMIT),
        name="post_tm" if time_major else "post_bm",
    )(y, bonus, sgr, glu, sgc, x, mod, s_conv, *layer_params, bones)


PROMPT_TILE = 256


def _block_ones():
    i = lax.broadcasted_iota(jnp.int32, (MXU_DIM, MXU_DIM), 0) // HEAD_DIM
    j = lax.broadcasted_iota(jnp.int32, (MXU_DIM, MXU_DIM), 1) // HEAD_DIM
    return (i == j).astype(BF16)


def _lower_tri():
    i = lax.broadcasted_iota(jnp.int32, (CHUNK, CHUNK), 0)
    j = lax.broadcasted_iota(jnp.int32, (CHUNK, CHUNK), 1)
    return (j <= i).astype(BF16)


def kernel(x_prompt, x_sample, c_prompt, c_sample, state_shift, state_wkv, state_conv, w_ada, b_ada, g_pre, g_post, w_in, mu, w0, w_up, a0, a_up, k_k, k_a, r_k, gn_r_g, gn_r_b, w_dw, b_dw, gn_c_g, gn_c_b, w_out):
    depth = w_in.shape[0]
    n_p, t_p, _ = x_prompt.shape
    n_s, t_s, _ = x_sample.shape
    bones = _block_ones()
    lt = _lower_tri()

    mod_p, mod_s = _mod_call(c_prompt, c_sample, w_ada, b_ada)
    mod_p = mod_p[:, :, None, :]

    xp = x_prompt
    xs = jnp.transpose(x_sample, (1, 0, 2))
    zeros_shift = jnp.zeros((n_p, 1, D_MODEL), F32)
    zeros_wkv = jnp.zeros((n_p, N_HEADS, HEAD_DIM, HEAD_DIM), F32)
    zeros_conv = jnp.zeros((n_p, HIST, D_CONV), F32)
    sconv_tm = jnp.transpose(state_conv, (0, 2, 1, 3))

    def dup_rows(a):
        a = jnp.transpose(a, (1, 0, 2))
        return jnp.concatenate([a, a], axis=1)

    rows = lambda a: a.reshape(depth, 1, -1)
    zl = jnp.zeros((depth, LORA, D_RWKV), F32)
    w_lora = jnp.concatenate(
        [jnp.concatenate([w_up, zl], axis=2), jnp.concatenate([zl, a_up], axis=2)],
        axis=1).astype(BF16)
    in_params = (rows(g_pre), w_in.astype(BF16), rows(mu), rows(w0), rows(a0), w_lora,
                 rows(k_k), rows(k_a), rows(r_k))
    post_params = (rows(gn_r_g), rows(gn_r_b), w_dw, rows(b_dw), rows(gn_c_g), rows(gn_c_b),
                   w_out.astype(BF16), rows(g_post))

    shifts_p, wkvs_p, convs_p, shifts_s, convs_s = [], [], [], [], []
    wkv_s = None
    for l in range(depth):
        (r, lw, km, v, kk, kka, bonus, sgr, glu, sgc, hl) = _inproj_call(
            False, l, xp, mod_p, zeros_shift, in_params, bones, INPROJ_TILE)
        y, s_new = _wkv_chunk_call(r, lw, km, v, kk, kka, zeros_wkv, lt)
        xp = _post_call(False, l, y, bonus, sgr, glu, sgc, xp, mod_p, zeros_conv, post_params,
                        bones, PROMPT_TILE)
        shifts_p.append(hl[:, 0])
        wkvs_p.append(s_new)
        convs_p.append(glu[:, -HIST:] if t_p >= HIST
                       else jnp.concatenate([zeros_conv, glu], axis=1)[:, -HIST:])

        (r, lw, km, v, kk, kka, bonus, sgr, glu, sgc, hl) = _inproj_call(
            True, l, xs, mod_s, state_shift, in_params, bones, None)
        y, wkv_s = _wkv_short_call(l, *(dup_rows(a) for a in (r, lw, km, v, kk, kka)),
                                   state_wkv, wkv_s)
        y = jnp.transpose(y[:, :SHORT_T], (1, 0, 2))
        xs = _post_call(True, l, y, bonus, sgr, glu, sgc, xs, mod_s, sconv_tm, post_params,
                        bones, None)
        shifts_s.append(hl)
        convs_s.append(jnp.concatenate(
            [state_conv[l][:, t_s:], jnp.transpose(glu, (1, 0, 2))], axis=1)[:, -HIST:])

    return (xp, jnp.transpose(xs, (1, 0, 2)),
            jnp.stack(shifts_p), jnp.stack(wkvs_p), jnp.stack(convs_p),
            jnp.stack(shifts_s), wkv_s, jnp.stack(convs_s))
```

```python
import functools

import jax
import jax.numpy as jnp
from jax import lax
from jax.experimental import pallas as pl
from jax.experimental.pallas import tpu as pltpu

F32 = jnp.float32
BF16 = jnp.bfloat16

D_MODEL = 1024
D_RWKV = 512
D_CONV = 512
HEAD_DIM = 64
N_HEADS = 8
N_PAIRS = N_HEADS // 2
LORA = 64
P_RWKV = 4 * D_RWKV + 2 * LORA
P_IN = P_RWKV + 3 * D_CONV
CONV_W = 31
HIST = CONV_W - 1
RMS_EPS = 1e-6
RWKV_GN_EPS = 64e-5
CONV_GN_EPS = 1e-5

LANES = 128
SUBLANES = 8
MXU_DIM = 256
CHUNK = 64
VMEM_LIMIT = 56 * 1024 * 1024

_NN = (((1,), (0,)), ((), ()))
_NT = (((1,), (1,)), ((), ()))
_TN = (((0,), (0,)), ((), ()))


def _dg(a, b, dims=_NN):
    return lax.dot_general(a, b, dims, preferred_element_type=F32)


def _split2(x):
    hi = x.astype(BF16)
    lo = (x - hi.astype(F32)).astype(BF16)
    return hi, lo


def _mm(a, b, dims=_NN, passes=1):
    if passes == 1:
        return _dg(a.astype(BF16), b.astype(BF16), dims)
    ah, al = _split2(a)
    bh, bl = _split2(b)
    return _dg(ah, bh, dims) + (_dg(ah, bl, dims) + _dg(al, bh, dims))


def _segsum(x, bones, split=False):
    pieces = _split2(x) if split else (x.astype(BF16),)
    cols = [slice(c0, c0 + MXU_DIM) for c0 in range(0, x.shape[1], MXU_DIM)]
    block = lambda s: functools.reduce(jnp.add, [_dg(p[:, s], bones) for p in pieces])
    return jnp.concatenate([block(s) for s in cols], axis=1)


def _sigmoid(x):
    return 1.0 / (1.0 + jnp.exp(-x))


def _group_norm(x, g, b, eps, bones):
    inv_n = 1.0 / HEAD_DIM
    mean = _segsum(x, bones, split=True) * inv_n
    d = x - mean
    var = _segsum(d * d, bones) * inv_n
    return d * lax.rsqrt(var + eps) * g + b


def _mod_body(cp_ref, cs_ref, w_ref, b_ref, op_ref, os_ref):
    w = w_ref[...].astype(BF16)
    for c_ref, o_ref in ((cp_ref, op_ref), (cs_ref, os_ref)):
        c = c_ref[...]
        o_ref[...] = _dg((c * _sigmoid(c)).astype(BF16), w) + b_ref[...]


def _mod_call(c_prompt, c_sample, w_ada, b_ada):
    depth, _, n3 = w_ada.shape
    bn = 768
    layer_cols = lambda l, j: (l, 0, j)
    whole = lambda a: pl.BlockSpec(a.shape, lambda l, j: (0, 0))
    out = lambda a: pl.BlockSpec((None, a.shape[0], bn), layer_cols)
    return pl.pallas_call(
        _mod_body,
        grid=(depth, n3 // bn),
        in_specs=[whole(c_prompt), whole(c_sample),
                  pl.BlockSpec((None, D_MODEL, bn), layer_cols),
                  pl.BlockSpec((None, 1, bn), layer_cols)],
        out_specs=[out(c_prompt), out(c_sample)],
        out_shape=[jax.ShapeDtypeStruct((depth, c.shape[0], n3), F32) for c in (c_prompt, c_sample)],
        compiler_params=pltpu.CompilerParams(
            dimension_semantics=("arbitrary", "arbitrary"), vmem_limit_bytes=VMEM_LIMIT),
        name="adaln_mod",
    )(c_prompt, c_sample, w_ada, b_ada.reshape(depth, 1, n3))


def _layer_spec(a, l, n_grid):
    idx = (l,) + (0,) * (a.ndim - 1)
    index_map = (lambda i: idx) if n_grid == 1 else (lambda i, t: idx)
    return pl.BlockSpec((None,) + a.shape[1:], index_map, pipeline_mode=pl.Buffered(1))


INPROJ_SUB = 256
INPROJ_TILE = 512


def _inproj_body(lag, x_ref, sh_ref, sc_ref, ss_ref, gpre_ref, win_ref, mu_ref,
                 w0_ref, a0_ref, wl_ref, kkp_ref, kap_ref, rkp_ref, bones_ref,
                 r_o, lw_o, km_o, v_o, kk_o, kka_o, bon_o, sgr_o, glu_o, sgc_o, hl_o, carry):
    step = pl.program_id(1)
    rows = x_ref.shape[1]
    sub = min(rows, INPROJ_SUB)
    bounds = [(r0, r0 + sub) for r0 in range(0, rows, sub)]
    pad = -(-lag // SUBLANES) * SUBLANES
    first = pad - lag

    @pl.when(step == 0)
    def _():
        ss = jnp.broadcast_to(ss_ref[...], (pad, D_MODEL)).astype(BF16)
        carry[first:pad, :] = _dg(ss, win_ref[:, :P_RWKV])[first:pad]

    def cols(hb, c0, c1):
        return _dg(hb, win_ref[:, c0:c1])

    def shifted(ur, r0, r1):
        carry[pad + r0:pad + r1, :] = ur
        return carry[first + r0:first + r1, :]

    def per_row(ref):
        v = ref[...]
        return v if lag == 1 else jnp.concatenate([v] * (sub // lag), axis=0)

    outs = (r_o, lw_o, km_o, v_o, kk_o, kka_o, bon_o, sgr_o, glu_o, sgc_o)
    params = (mu_ref, w0_ref, a0_ref, wl_ref, kkp_ref, kap_ref, rkp_ref, bones_ref)
    half = 2 * D_RWKV
    pending = iter(())
    for r0, r1 in bounds:
        x = x_ref[0, r0:r1]
        ms = jnp.mean(x * x, axis=-1, keepdims=True)
        h = x * lax.rsqrt(ms + RMS_EPS) * gpre_ref[...]
        h = h * (1.0 + per_row(sc_ref)) + per_row(sh_ref)
        if r1 == rows:
            hl_o[...] = h[sub - lag:sub]
        hb = h.astype(BF16)
        ur_a = cols(hb, 0, half)
        next(pending, None)
        ur = jnp.concatenate([ur_a, cols(hb, half, P_RWKV)], axis=1)
        next(pending, None)
        conv_blocks = [cols(hb, P_RWKV, P_RWKV + D_CONV)]
        next(pending, None)
        conv_blocks += [cols(hb, P_RWKV + j * D_CONV, P_RWKV + (j + 1) * D_CONV) for j in (1, 2)]
        for _ in pending:
            pass
        prev = shifted(ur, r0, r1)
        pending = _inproj_rows(ur, prev, conv_blocks, slice(r0, r1), params, outs)
    for _ in pending:
        pass
    carry[first:pad, :] = ur[sub - lag:sub]


def _inproj_rows(ur, prev, conv_blocks, rs, params, outs):
    mu_ref, w0_ref, a0_ref, wl_ref, kkp_ref, kap_ref, rkp_ref, bones_ref = params
    r_o, lw_o, km_o, v_o, kk_o, kka_o, bon_o, sgr_o, glu_o, sgc_o = outs
    glu_a, glu_b, g_c = conv_blocks
    urs = ur + (prev - ur) * mu_ref[...]
    r = urs[:, 0:D_RWKV]
    k = urs[:, D_RWKV:2 * D_RWKV]
    v = urs[:, 2 * D_RWKV:3 * D_RWKV]
    g_r = urs[:, 3 * D_RWKV:4 * D_RWKV]
    lora = urs[:, 4 * D_RWKV:P_RWKV]
    lane = lax.broadcasted_iota(jnp.int32, lora.shape, 1)
    lora_in = jnp.where(lane < LORA, jnp.tanh(lora), lora).astype(BF16)
    wa = _dg(lora_in, wl_ref[...])
    r_o[0, rs] = r
    v_o[0, rs] = v
    glu_o[0, rs] = glu_a * _sigmoid(glu_b)
    sgc_o[0, rs] = g_c * _sigmoid(g_c)
    yield

    z = -(w0_ref[...] + wa[:, :D_RWKV])
    softplus = jnp.maximum(z, 0.0) + jnp.log(1.0 + jnp.exp(-jnp.abs(z)))
    lw = -jnp.exp(-softplus - 0.5)
    a = _sigmoid(a0_ref[...] + wa[:, D_RWKV:])
    lw_o[0, rs] = lw
    bones = bones_ref[...]
    kkr = k * kkp_ref[...]
    ssq = _segsum(kkr * kkr, bones)
    yield

    kk = kkr * lax.rsqrt(jnp.maximum(ssq, 1e-24))
    kk_o[0, rs] = kk
    kka_o[0, rs] = kk * a
    km = k * (1.0 + (a - 1.0) * kap_ref[...])
    km_o[0, rs] = km
    rkk = _segsum(r * km * rkp_ref[...], bones)
    yield

    bon_o[0, rs] = rkk * v
    sgr_o[0, rs] = g_r * _sigmoid(g_r)


def _inproj_call(time_major, l, x, mod, s_shift, layer_params, bones, tile):
    if time_major:
        n_t, lag, _ = x.shape
        x = x.reshape(1, n_t * lag, D_MODEL)
        n_outer, n_inner, rows = 1, n_t * lag, n_t * lag
        grid = (1, 1)
        x_map = lambda i, t: (0, 0, 0)
        mod_spec = lambda j: pl.BlockSpec((None, lag, D_MODEL), lambda i, t: (l, 0, j))
        shift_spec = _layer_spec(s_shift, l, 2)
        hl_shape = (lag, D_MODEL)
        hl_spec = pl.BlockSpec(hl_shape, lambda i, t: (0, 0))
    else:
        n_outer, n_inner, _ = x.shape
        lag = 1
        rows = tile
        grid = (n_outer, n_inner // rows)
        x_map = lambda b, t: (b, t, 0)
        mod_spec = lambda j: pl.BlockSpec((None, None, 1, D_MODEL), lambda b, t: (l, b, 0, j))
        shift_spec = pl.BlockSpec((None, 1, D_MODEL), lambda b, t: (b, 0, 0))
        hl_shape = (n_outer, 1, D_MODEL)
        hl_spec = pl.BlockSpec((None, 1, D_MODEL), lambda b, t: (b, 0, 0))
    carry = pltpu.VMEM((-(-lag // SUBLANES) * SUBLANES + rows, P_RWKV), F32)
    out_block = pl.BlockSpec((1, rows, D_RWKV), x_map)
    out_sds = jax.ShapeDtypeStruct((n_outer, n_inner, D_RWKV), F32)
    return pl.pallas_call(
        functools.partial(_inproj_body, lag),
        grid=grid,
        in_specs=[pl.BlockSpec((1, rows, D_MODEL), x_map), mod_spec(0), mod_spec(1), shift_spec]
        + [_layer_spec(p, l, 2) for p in layer_params]
        + [pl.BlockSpec(bones.shape, lambda i, t: (0, 0))],
        out_specs=[out_block] * 10 + [hl_spec],
        out_shape=[out_sds] * 10 + [jax.ShapeDtypeStruct(hl_shape, F32)],
        scratch_shapes=[carry],
        compiler_params=pltpu.CompilerParams(
            dimension_semantics=("arbitrary", "arbitrary"), vmem_limit_bytes=VMEM_LIMIT),
        name="inproj_tm" if time_major else "inproj_bm",
    )(x, mod, mod, s_shift, *layer_params, bones)


PASS_GRAM = 1
PASS_INV = 1
PASS_STATE = 1
PASS_OUT = 1
CHUNK_SEQS = 8


def _wkv_chunk_body(r_ref, lw_ref, km_ref, v_ref, kk_ref, kka_ref, s0_ref, lt_ref,
                    y_ref, so_ref, s_scr):
    c = pl.program_id(1)
    n_seq = r_ref.shape[0]
    zeros_hh = jnp.zeros((HEAD_DIM, HEAD_DIM), F32)

    @pl.when(c == 0)
    def _():
        for i in range(n_seq):
            for p in range(N_PAIRS):
                top = jnp.concatenate([s0_ref[i, 2 * p], zeros_hh], axis=1)
                bot = jnp.concatenate([zeros_hh, s0_ref[i, 2 * p + 1]], axis=1)
                s_scr[i * N_PAIRS + p] = jnp.concatenate([top, bot], axis=0)

    lt = lt_ref[...]
    lane = lax.broadcasted_iota(jnp.int32, (CHUNK, LANES), 1)
    first = lane < HEAD_DIM
    row = lax.broadcasted_iota(jnp.int32, (LANES, LANES), 0)
    col = lax.broadcasted_iota(jnp.int32, (LANES, LANES), 1)
    same_head = (row < HEAD_DIM) == (col < HEAD_DIM)
    rt_i = row & (HEAD_DIM - 1)
    ct_i = col & (HEAD_DIM - 1)
    tri_strict = same_head & (ct_i < rt_i)
    tri_incl = same_head & (ct_i <= rt_i)
    eye = (row == col).astype(F32)

    def stack(xp):
        return jnp.concatenate([jnp.where(first, xp, 0.0), jnp.where(first, 0.0, xp)], axis=0)

    x2, y2, vs, hk, p_last = [], [], [], [], []
    for i in range(n_seq):
        lw = lw_ref[i]
        l_hi = lw.astype(BF16)
        l_r1 = lw - l_hi.astype(F32)
        l_mid = l_r1.astype(BF16)
        l_lo = (l_r1 - l_mid.astype(F32)).astype(BF16)
        cum = _dg(lt, l_hi) + (_dg(lt, l_mid) + _dg(lt, l_lo))
        cum_last = cum[CHUNK - 1:CHUNK, :]
        p_inv = jnp.exp(-cum)
        p_end = jnp.exp(cum_last - cum)
        at_all = -kk_ref[i] * jnp.exp(cum - lw)
        rt_all = r_ref[i] * jnp.exp(cum)
        kka = kka_ref[i]
        km = km_ref[i]
        bt_all = kka * p_inv
        kt_all = km * p_inv
        bh_all = kka * p_end
        kh_all = km * p_end
        v_all = v_ref[i]
        pl_all = jnp.exp(cum_last)
        for p in range(N_PAIRS):
            s = slice(LANES * p, LANES * (p + 1))
            x2.append(jnp.concatenate([stack(at_all[:, s]), stack(rt_all[:, s])], axis=0))
            y2.append(jnp.concatenate([stack(bt_all[:, s]), stack(kt_all[:, s])], axis=0))
            vs.append(stack(v_all[:, s]))
            hk.append(jnp.concatenate([stack(bh_all[:, s]), stack(kh_all[:, s])], axis=0))
            p_last.append(pl_all[:, s])

    units = range(n_seq * N_PAIRS)
    g = [_mm(x2[u], y2[u], _NT, PASS_GRAM) for u in units]
    n_pow = [jnp.where(tri_strict, g[u][:LANES, :LANES], 0.0) for u in units]
    a_ak = [jnp.where(tri_strict, g[u][:LANES, LANES:], 0.0) for u in units]
    a_r = [jnp.concatenate([jnp.where(tri_incl, g[u][LANES:, :LANES], 0.0),
                            jnp.where(tri_incl, g[u][LANES:, LANES:], 0.0)], axis=1) for u in units]

    t_inv = [eye + n_pow[u] for u in units]
    span = 2
    while span < CHUNK:
        n_pow = [_mm(n_pow[u], n_pow[u], _NN, PASS_INV) for u in units]
        t_inv = [t_inv[u] + _mm(t_inv[u], n_pow[u], _NN, PASS_INV) for u in units]
        span *= 2

    s_bd = [s_scr[u] for u in units]
    xs = [_mm(x2[u], s_bd[u], _NT, PASS_STATE) for u in units]
    akv = [_mm(a_ak[u], vs[u], _NN, PASS_STATE) for u in units]
    z_mat = [_mm(t_inv[u], xs[u][:LANES] + akv[u], _NN, PASS_STATE) for u in units]
    zv = [jnp.concatenate([z_mat[u], vs[u]], axis=0) for u in units]
    y_bd = [xs[u][LANES:] + _mm(a_r[u], zv[u], _NN, PASS_OUT) for u in units]
    s_new = [s_bd[u] * p_last[u] + _mm(zv[u], hk[u], _TN, PASS_STATE) for u in units]
    for u in units:
        i, p = divmod(u, N_PAIRS)
        y_ref[i, :, LANES * p:LANES * (p + 1)] = y_bd[u][:HEAD_DIM] + y_bd[u][HEAD_DIM:]
        s_scr[u] = s_new[u]
        so_ref[i, 2 * p] = s_new[u][:HEAD_DIM, :HEAD_DIM]
        so_ref[i, 2 * p + 1] = s_new[u][HEAD_DIM:, HEAD_DIM:]


def _wkv_chunk_call(r, lw, km, v, kk, kka, s0, lt):
    n_b, n_t, _ = r.shape
    nq = CHUNK_SEQS
    seq_block = pl.BlockSpec((nq, CHUNK, D_RWKV), lambda b, c: (b, c, 0))
    st_block = pl.BlockSpec((nq, N_HEADS, HEAD_DIM, HEAD_DIM), lambda b, c: (b, 0, 0, 0))
    return pl.pallas_call(
        _wkv_chunk_body,
        grid=(n_b // nq, n_t // CHUNK),
        in_specs=[seq_block] * 6 + [st_block, pl.BlockSpec(lt.shape, lambda b, c: (0, 0))],
        out_specs=[seq_block, st_block],
        out_shape=[jax.ShapeDtypeStruct(r.shape, F32), jax.ShapeDtypeStruct(s0.shape, F32)],
        scratch_shapes=[pltpu.VMEM((nq * N_PAIRS, LANES, LANES), F32)],
        compiler_params=pltpu.CompilerParams(
            dimension_semantics=("arbitrary", "arbitrary"), vmem_limit_bytes=VMEM_LIMIT),
        name="wkv_chunk",
    )(r, lw, km, v, kk, kka, s0, lt)


SHORT_T = 4
SHORT_SEQS = LANES // (2 * SHORT_T)


def _wkv_short_body(layer, r_ref, lw_ref, km_ref, v_ref, kk_ref, kka_ref, s_all_ref, *rest):
    y_ref, so_all_ref = rest[-2:]
    if layer is None:
        s_ref, so_ref = s_all_ref, so_all_ref
    else:
        s_ref, so_ref = s_all_ref.at[layer], so_all_ref.at[layer]
        for other in range(s_all_ref.shape[0]):
            if other != layer:
                so_all_ref[other] = s_all_ref[other]
    n_seq = s_ref.shape[0]
    grp = 2 * SHORT_T
    row = lax.broadcasted_iota(jnp.int32, (LANES, LANES), 0)
    col = lax.broadcasted_iota(jnp.int32, (LANES, LANES), 1)
    same_blk = (row // SHORT_T) == (col // SHORT_T)
    t_row = row % SHORT_T
    t_col = col % SHORT_T
    strict = same_blk & (t_col < t_row)
    incl = same_blk & (t_col <= t_row)
    eye = (row == col).astype(F32)
    sum_mat = jnp.concatenate([incl.astype(BF16), same_blk.astype(BF16)], axis=0)
    own = ((row // SHORT_T) % 2) == (col // HEAD_DIM)
    zeros_hh = jnp.zeros((HEAD_DIM, HEAD_DIM), F32)

    def rows_of(ref, p):
        return ref[:, :, LANES * p:LANES * (p + 1)].reshape(n_seq * grp, LANES)

    def mask(x):
        return jnp.where(own, x, 0.0)

    pairs = range(N_PAIRS)
    x2, y2, vs, bh, kh, p_last = [], [], [], [], [], []
    for p in pairs:
        lw = rows_of(lw_ref, p)
        l_hi = lw.astype(BF16)
        l_r1 = lw - l_hi.astype(F32)
        l_mid = l_r1.astype(BF16)
        l_lo = (l_r1 - l_mid.astype(F32)).astype(BF16)
        sums = _dg(sum_mat, l_hi) + (_dg(sum_mat, l_mid) + _dg(sum_mat, l_lo))
        cum = sums[:LANES]
        tot = sums[LANES:]
        p_inv = jnp.exp(-cum)
        p_end = jnp.exp(tot - cum)
        kka = rows_of(kka_ref, p)
        km = rows_of(km_ref, p)
        x2.append(jnp.concatenate([mask(-rows_of(kk_ref, p) * jnp.exp(cum - lw)),
                                   mask(rows_of(r_ref, p) * jnp.exp(cum))], axis=0))
        y2.append(jnp.concatenate([mask(kka * p_inv), mask(km * p_inv)], axis=0))
        vs.append(mask(rows_of(v_ref, p)))
        bh.append(mask(kka * p_end))
        kh.append(mask(km * p_end))
        p_last.append(jnp.exp(tot))

    g = [_mm(x2[p], y2[p], _NT, PASS_GRAM) for p in pairs]
    n_mat = [jnp.where(strict, g[p][:LANES, :LANES], 0.0) for p in pairs]
    a_ak = [jnp.where(strict, g[p][:LANES, LANES:], 0.0) for p in pairs]
    a_r = [jnp.concatenate([jnp.where(incl, g[p][LANES:, :LANES], 0.0),
                            jnp.where(incl, g[p][LANES:, LANES:], 0.0)], axis=1) for p in pairs]
    n_sq = [_mm(n_mat[p], n_mat[p], _NN, PASS_INV) for p in pairs]
    t_inv = [(eye + n_mat[p]) + _mm(eye + n_mat[p], n_sq[p], _NN, PASS_INV) for p in pairs]
    akv = [_mm(a_ak[p], vs[p], _NN, PASS_STATE) for p in pairs]

    def seq_rows(x, i):
        return x[grp * i:grp * (i + 1)]

    s_bd, xa_s, xr_s = [], [], []
    for p in pairs:
        s_p, xa_p, xr_p = [], [], []
        for i in range(n_seq):
            top = jnp.concatenate([s_ref[i, 2 * p], zeros_hh], axis=1)
            bot = jnp.concatenate([zeros_hh, s_ref[i, 2 * p + 1]], axis=1)
            s_i = jnp.concatenate([top, bot], axis=0)
            x_i = jnp.concatenate([seq_rows(x2[p][:LANES], i), seq_rows(x2[p][LANES:], i)], axis=0)
            xs_i = _mm(x_i, s_i, _NT, PASS_STATE)
            s_p.append(s_i)
            xa_p.append(xs_i[:grp])
            xr_p.append(xs_i[grp:])
        s_bd.append(s_p)
        xa_s.append(jnp.concatenate(xa_p, axis=0))
        xr_s.append(jnp.concatenate(xr_p, axis=0))

    z_mat = [_mm(t_inv[p], xa_s[p] + akv[p], _NN, PASS_STATE) for p in pairs]
    zv = [jnp.concatenate([z_mat[p], vs[p]], axis=0) for p in pairs]
    y_bd = [xr_s[p] + _mm(a_r[p], zv[p], _NN, PASS_OUT) for p in pairs]
    for p in pairs:
        y3 = y_bd[p].reshape(n_seq, grp, LANES)
        y_ref[:, :, LANES * p:LANES * (p + 1)] = y3 + jnp.concatenate(
            [y3[:, SHORT_T:], y3[:, :SHORT_T]], axis=1)
        for i in range(n_seq):
            zv_i = jnp.concatenate([seq_rows(z_mat[p], i), seq_rows(vs[p], i)], axis=0)
            hk_i = jnp.concatenate([seq_rows(bh[p], i), seq_rows(kh[p], i)], axis=0)
            s_new = (s_bd[p][i] * p_last[p][grp * i:grp * i + 1]
                     + _mm(zv_i, hk_i, _TN, PASS_STATE))
            so_ref[i, 2 * p] = s_new[:HEAD_DIM, :HEAD_DIM]
            so_ref[i, 2 * p + 1] = s_new[HEAD_DIM:, HEAD_DIM:]


def _wkv_short_call(l, r, lw, km, v, kk, kka, s_all, s_new_all):
    n_b = r.shape[0]
    nq = SHORT_SEQS
    seq_block = pl.BlockSpec((nq, 2 * SHORT_T, D_RWKV), lambda i: (i, 0, 0))
    carried = [] if s_new_all is None else [s_new_all]
    if carried:
        st_block = pl.BlockSpec((None, nq, N_HEADS, HEAD_DIM, HEAD_DIM), lambda i: (l, i, 0, 0, 0))
    else:
        st_block = pl.BlockSpec((s_all.shape[0], nq, N_HEADS, HEAD_DIM, HEAD_DIM),
                                lambda i: (0, i, 0, 0, 0))
    return pl.pallas_call(
        functools.partial(_wkv_short_body, None if carried else l),
        grid=(n_b // nq,),
        in_specs=[seq_block] * 6 + [st_block] + [pl.BlockSpec(memory_space=pl.ANY)] * len(carried),
        out_specs=[seq_block, st_block],
        out_shape=[jax.ShapeDtypeStruct(r.shape, F32), jax.ShapeDtypeStruct(s_all.shape, F32)],
        input_output_aliases={7: 1} if carried else {},
        compiler_params=pltpu.CompilerParams(
            dimension_semantics=("arbitrary",), vmem_limit_bytes=VMEM_LIMIT),
        name="wkv_short",
    )(r, lw, km, v, kk, kka, s_all, *carried)


CONV_ROW_BLOCK = 32
HIST_PAD = 32


def _merge_and_residual(y, bonus, sgr, conv, sgc, x, gate, gnr_g, gnr_b, gnc_g, gnc_b,
                        wout_ref, gpost, bones):
    y_r = (_group_norm(y, gnr_g, gnr_b, RWKV_GN_EPS, bones) + bonus) * sgr
    cn = _group_norm(conv, gnc_g, gnc_b, CONV_GN_EPS, bones)
    y_c = cn * _sigmoid(cn) * sgc
    mix = _dg(y_r.astype(BF16), wout_ref[0:D_RWKV, :]) + _dg(y_c.astype(BF16), wout_ref[D_RWKV:, :])
    ms = jnp.mean(mix * mix, axis=-1, keepdims=True)
    return x + gate * (mix * lax.rsqrt(ms + RMS_EPS) * gpost)


def _post_bm_body(y_ref, bon_ref, sgr_ref, glu_ref, sgc_ref, x_ref, gate_ref, sconv_ref,
                  gnrg_ref, gnrb_ref, wdw_ref, bdw_ref, gncg_ref, gncb_ref, wout_ref, gpost_ref,
                  bones_ref, xo_ref, buf, conv):
    step = pl.program_id(1)
    rows = y_ref.shape[1]
    off = HIST_PAD - HIST
    shifted_rows = rows + HIST_PAD - SUBLANES

    @pl.when(step == 0)
    def _():
        buf[0, 0:off, :] = jnp.zeros((off, D_CONV), F32)
        buf[0, off:HIST_PAD, :] = sconv_ref[0]

    buf[0, HIST_PAD:HIST_PAD + rows, :] = glu_ref[0]
    for q in range(1, SUBLANES):
        buf[q, 0:shifted_rows, :] = buf[0, q:q + shifted_rows, :]
    for r0 in range(0, rows, CONV_ROW_BLOCK):
        acc = jnp.broadcast_to(bdw_ref[...], (CONV_ROW_BLOCK, D_CONV))
        for j in range(CONV_W):
            q = (j + off) % SUBLANES
            base = r0 + j + off - q
            acc = acc + wdw_ref[j:j + 1, :] * buf[q, base:base + CONV_ROW_BLOCK, :]
        conv[r0:r0 + CONV_ROW_BLOCK, :] = acc
    buf[0, off:HIST_PAD, :] = buf[0, rows + off:rows + HIST_PAD, :]

    xo_ref[0] = _merge_and_residual(
        y_ref[0], bon_ref[0], sgr_ref[0], conv[...], sgc_ref[0], x_ref[0], gate_ref[...],
        gnrg_ref[...], gnrb_ref[...], gncg_ref[...], gncb_ref[...], wout_ref, gpost_ref[...],
        bones_ref[...])


def _post_tm_body(y_ref, bon_ref, sgr_ref, glu_ref, sgc_ref, x_ref, gate_ref, sconv_ref,
                  gnrg_ref, gnrb_ref, wdw_ref, bdw_ref, gncg_ref, gncb_ref, wout_ref, gpost_ref,
                  bones_ref, xo_ref, co_ref):
    n_t, rows, _ = y_ref.shape
    for i in range(n_t, n_t + HIST):
        co_ref[i - n_t] = sconv_ref[i] if i < HIST else glu_ref[i - HIST]
    convs = []
    for t in range(n_t):
        acc = jnp.broadcast_to(bdw_ref[...], (rows, D_CONV))
        for j in range(CONV_W):
            i = t + j
            src = sconv_ref[i] if i < HIST else glu_ref[i - HIST]
            acc = acc + wdw_ref[j:j + 1, :] * src
        convs.append(acc)
    flat = lambda ref: ref[...].reshape(n_t * rows, ref.shape[-1])
    out = _merge_and_residual(
        flat(y_ref), flat(bon_ref), flat(sgr_ref), jnp.concatenate(convs, axis=0), flat(sgc_ref),
        flat(x_ref), jnp.concatenate([gate_ref[...]] * n_t, axis=0),
        gnrg_ref[...], gnrb_ref[...], gncg_ref[...], gncb_ref[...], wout_ref, gpost_ref[...],
        bones_ref[...])
    xo_ref[...] = out.reshape(n_t, rows, D_MODEL)


def _post_call(time_major, l, y, bonus, sgr, glu, sgc, x, mod, s_conv, layer_params, bones, tile):
    n_outer, n_inner, _ = x.shape
    if time_major:
        rows = n_inner
        grid = (1, 1)
        lead = n_outer
        x_map = lambda i, t: (0, 0, 0)
        gate_spec = pl.BlockSpec((None, rows, D_MODEL), lambda i, t: (l, 0, 2))
        sconv_spec = _layer_spec(s_conv, l, 2)
        scratch = []
        body = _post_tm_body
    else:
        rows = tile
        grid = (n_outer, n_inner // rows)
        lead = 1
        x_map = lambda b, t: (b, t, 0)
        gate_spec = pl.BlockSpec((None, None, 1, D_MODEL), lambda b, t: (l, b, 0, 2))
        sconv_spec = pl.BlockSpec((1, HIST, D_CONV), lambda b, t: (b, 0, 0))
        scratch = [pltpu.VMEM((SUBLANES, HIST_PAD + rows, D_CONV), F32),
                   pltpu.VMEM((rows, D_CONV), F32)]
        body = _post_bm_body
    half = pl.BlockSpec((lead, rows, D_RWKV), x_map)
    full = pl.BlockSpec((lead, rows, D_MODEL), x_map)
    out_specs, out_shape = full, jax.ShapeDtypeStruct(x.shape, F32)
    if time_major:
        out_specs = [full, pl.BlockSpec(s_conv.shape[1:], lambda i, t: (0, 0, 0))]
        out_shape = [out_shape, jax.ShapeDtypeStruct(s_conv.shape[1:], F32)]
    return pl.pallas_call(
        body,
        grid=grid,
        in_specs=[half] * 5 + [full, gate_spec, sconv_spec]
        + [_layer_spec(p, l, 2) for p in layer_params]
        + [pl.BlockSpec(bones.shape, lambda i, t: (0, 0))],
        out_specs=out_specs,
        out_shape=out_shape,
        scratch_shapes=scratch,
        compiler_params=pltpu.CompilerParams(
            dimension_semantics=("arbitrary", "arbitrary"), vmem_limit_bytes=VMEM_LIMIT),
        name="post_tm" if time_major else "post_bm",
    )(y, bonus, sgr, glu, sgc, x, mod, s_conv, *layer_params, bones)


POST_TILE = 512


def _block_ones():
    i = lax.broadcasted_iota(jnp.int32, (MXU_DIM, MXU_DIM), 0) // HEAD_DIM
    j = lax.broadcasted_iota(jnp.int32, (MXU_DIM, MXU_DIM), 1) // HEAD_DIM
    return (i == j).astype(BF16)


def _lower_tri():
    i = lax.broadcasted_iota(jnp.int32, (CHUNK, CHUNK), 0)
    j = lax.broadcasted_iota(jnp.int32, (CHUNK, CHUNK), 1)
    return (j <= i).astype(BF16)


def kernel(x_prompt, x_sample, c_prompt, c_sample, state_shift, state_wkv, state_conv, w_ada, b_ada, g_pre, g_post, w_in, mu, w0, w_up, a0, a_up, k_k, k_a, r_k, gn_r_g, gn_r_b, w_dw, b_dw, gn_c_g, gn_c_b, w_out):
    depth = w_in.shape[0]
    n_p, t_p, _ = x_prompt.shape
    n_s, t_s, _ = x_sample.shape
    bones = _block_ones()
    lt = _lower_tri()

    mod_p, mod_s = _mod_call(c_prompt, c_sample, w_ada, b_ada)
    mod_p = mod_p[:, :, None, :]

    xp = x_prompt
    xs = jnp.transpose(x_sample, (1, 0, 2))
    zeros_shift = jnp.zeros((n_p, 1, D_MODEL), F32)
    zeros_wkv = jnp.zeros((n_p, N_HEADS, HEAD_DIM, HEAD_DIM), F32)
    zeros_conv = jnp.zeros((n_p, HIST, D_CONV), F32)
    sconv_tm = jnp.transpose(state_conv, (0, 2, 1, 3))

    def dup_rows(a):
        a = jnp.transpose(a, (1, 0, 2))
        return jnp.concatenate([a, a], axis=1)

    rows = lambda a: a.reshape(depth, 1, -1)
    zl = jnp.zeros((depth, LORA, D_RWKV), F32)
    w_lora = jnp.concatenate(
        [jnp.concatenate([w_up, zl], axis=2), jnp.concatenate([zl, a_up], axis=2)],
        axis=1).astype(BF16)
    in_params = (rows(g_pre), w_in.astype(BF16), rows(mu), rows(w0), rows(a0), w_lora,
                 rows(k_k), rows(k_a), rows(r_k))
    post_params = (rows(gn_r_g), rows(gn_r_b), w_dw, rows(b_dw), rows(gn_c_g), rows(gn_c_b),
                   w_out.astype(BF16), rows(g_post))

    shifts_p, wkvs_p, convs_p, shifts_s, convs_s = [], [], [], [], []
    wkv_s = None
    for l in range(depth):
        (r, lw, km, v, kk, kka, bonus, sgr, glu, sgc, hl) = _inproj_call(
            False, l, xp, mod_p, zeros_shift, in_params, bones, INPROJ_TILE)
        y, s_new = _wkv_chunk_call(r, lw, km, v, kk, kka, zeros_wkv, lt)
        xp = _post_call(False, l, y, bonus, sgr, glu, sgc, xp, mod_p, zeros_conv, post_params,
                        bones, POST_TILE)
        shifts_p.append(hl[:, 0])
        wkvs_p.append(s_new)
        convs_p.append(glu[:, -HIST:] if t_p >= HIST
                       else jnp.concatenate([zeros_conv, glu], axis=1)[:, -HIST:])

        *seq_outs, hl = _inproj_call(True, l, xs, mod_s, state_shift, in_params, bones, None)
        (r, lw, km, v, kk, kka, bonus, sgr, glu, sgc) = (
            a.reshape(t_s, n_s, D_RWKV) for a in seq_outs)
        y, wkv_s = _wkv_short_call(l, *(dup_rows(a) for a in (r, lw, km, v, kk, kka)),
                                   state_wkv, wkv_s)
        y = jnp.transpose(y[:, :SHORT_T], (1, 0, 2))
        xs, conv_new = _post_call(True, l, y, bonus, sgr, glu, sgc, xs, mod_s, sconv_tm,
                                  post_params, bones, None)
        shifts_s.append(hl)
        convs_s.append(conv_new)

    return (xp, jnp.transpose(xs, (1, 0, 2)),
            jnp.stack(shifts_p), jnp.stack(wkvs_p), jnp.stack(convs_p),
            jnp.stack(shifts_s), wkv_s, jnp.transpose(jnp.stack(convs_s), (0, 2, 1, 3)))
```

```python
import functools

import jax
import jax.numpy as jnp
from jax import lax
from jax.experimental import pallas as pl
from jax.experimental.pallas import tpu as pltpu

F32 = jnp.float32
BF16 = jnp.bfloat16

D_MODEL = 1024
D_RWKV = 512
D_CONV = 512
HEAD_DIM = 64
N_HEADS = 8
N_PAIRS = N_HEADS // 2
LORA = 64
P_RWKV = 4 * D_RWKV + 2 * LORA
P_IN = P_RWKV + 3 * D_CONV
CONV_W = 31
HIST = CONV_W - 1
RMS_EPS = 1e-6
RWKV_GN_EPS = 64e-5
CONV_GN_EPS = 1e-5

LANES = 128
SUBLANES = 8
MXU_DIM = 256
CHUNK = 64
VMEM_LIMIT = 56 * 1024 * 1024

_NN = (((1,), (0,)), ((), ()))
_NT = (((1,), (1,)), ((), ()))
_TN = (((0,), (0,)), ((), ()))


def _dg(a, b, dims=_NN):
    return lax.dot_general(a, b, dims, preferred_element_type=F32)


def _split2(x):
    hi = x.astype(BF16)
    lo = (x - hi.astype(F32)).astype(BF16)
    return hi, lo


def _mm(a, b, dims=_NN, passes=1):
    if passes == 1:
        return _dg(a.astype(BF16), b.astype(BF16), dims)
    ah, al = _split2(a)
    bh, bl = _split2(b)
    return _dg(ah, bh, dims) + (_dg(ah, bl, dims) + _dg(al, bh, dims))


def _segsum(x, bones, split=False):
    pieces = _split2(x) if split else (x.astype(BF16),)
    cols = [slice(c0, c0 + MXU_DIM) for c0 in range(0, x.shape[1], MXU_DIM)]
    block = lambda s: functools.reduce(jnp.add, [_dg(p[:, s], bones) for p in pieces])
    return jnp.concatenate([block(s) for s in cols], axis=1)


def _sigmoid(x):
    return 1.0 / (1.0 + jnp.exp(-x))


def _group_norm(x, g, b, eps, bones):
    inv_n = 1.0 / HEAD_DIM
    mean = _segsum(x, bones, split=True) * inv_n
    d = x - mean
    var = _segsum(d * d, bones) * inv_n
    return d * lax.rsqrt(var + eps) * g + b


def _mod_body(cp_ref, cs_ref, w_ref, b_ref, op_ref, os_ref):
    w = w_ref[...].astype(BF16)
    for c_ref, o_ref in ((cp_ref, op_ref), (cs_ref, os_ref)):
        c = c_ref[...]
        o_ref[...] = _dg((c * _sigmoid(c)).astype(BF16), w) + b_ref[...]


def _mod_call(c_prompt, c_sample, w_ada, b_ada):
    depth, _, n3 = w_ada.shape
    bn = 768
    layer_cols = lambda l, j: (l, 0, j)
    whole = lambda a: pl.BlockSpec(a.shape, lambda l, j: (0, 0))
    out = lambda a: pl.BlockSpec((None, a.shape[0], bn), layer_cols)
    return pl.pallas_call(
        _mod_body,
        grid=(depth, n3 // bn),
        in_specs=[whole(c_prompt), whole(c_sample),
                  pl.BlockSpec((None, D_MODEL, bn), layer_cols),
                  pl.BlockSpec((None, 1, bn), layer_cols)],
        out_specs=[out(c_prompt), out(c_sample)],
        out_shape=[jax.ShapeDtypeStruct((depth, c.shape[0], n3), F32) for c in (c_prompt, c_sample)],
        compiler_params=pltpu.CompilerParams(
            dimension_semantics=("arbitrary", "arbitrary"), vmem_limit_bytes=VMEM_LIMIT),
        name="adaln_mod",
    )(c_prompt, c_sample, w_ada, b_ada.reshape(depth, 1, n3))


def _layer_spec(a, l, n_grid):
    idx = (l,) + (0,) * (a.ndim - 1)
    index_map = (lambda i: idx) if n_grid == 1 else (lambda i, t: idx)
    return pl.BlockSpec((None,) + a.shape[1:], index_map, pipeline_mode=pl.Buffered(1))


INPROJ_SUB = 256
INPROJ_TILE = 512


def _inproj_body(lag, x_ref, sh_ref, sc_ref, ss_ref, gpre_ref, win_ref, mu_ref,
                 w0_ref, a0_ref, wl_ref, kkp_ref, kap_ref, rkp_ref, bones_ref,
                 r_o, lw_o, km_o, v_o, kk_o, kka_o, bon_o, sgr_o, glu_o, sgc_o, hl_o, carry):
    step = pl.program_id(1)
    rows = x_ref.shape[1]
    sub = min(rows, INPROJ_SUB)
    bounds = [(r0, r0 + sub) for r0 in range(0, rows, sub)]
    pad = -(-lag // SUBLANES) * SUBLANES
    first = pad - lag

    @pl.when(step == 0)
    def _():
        ss = jnp.broadcast_to(ss_ref[...], (pad, D_MODEL)).astype(BF16)
        carry[first:pad, :] = _dg(ss, win_ref[:, :P_RWKV])[first:pad]

    def cols(hb, c0, c1):
        return _dg(hb, win_ref[:, c0:c1])

    def shifted(ur, r0, r1):
        carry[pad + r0:pad + r1, :] = ur
        return carry[first + r0:first + r1, :]

    def per_row(ref):
        v = ref[...]
        return v if lag == 1 else jnp.concatenate([v] * (sub // lag), axis=0)

    outs = (r_o, lw_o, km_o, v_o, kk_o, kka_o, bon_o, sgr_o, glu_o, sgc_o)
    params = (mu_ref, w0_ref, a0_ref, wl_ref, kkp_ref, kap_ref, rkp_ref, bones_ref)
    half = 2 * D_RWKV
    pending = iter(())
    for r0, r1 in bounds:
        x = x_ref[0, r0:r1]
        ms = jnp.mean(x * x, axis=-1, keepdims=True)
        h = x * lax.rsqrt(ms + RMS_EPS) * gpre_ref[...]
        h = h * (1.0 + per_row(sc_ref)) + per_row(sh_ref)
        if r1 == rows:
            hl_o[...] = h[sub - lag:sub]
        hb = h.astype(BF16)
        ur_a = cols(hb, 0, half)
        next(pending, None)
        ur = jnp.concatenate([ur_a, cols(hb, half, P_RWKV)], axis=1)
        next(pending, None)
        conv_blocks = [cols(hb, P_RWKV, P_RWKV + D_CONV)]
        next(pending, None)
        conv_blocks += [cols(hb, P_RWKV + j * D_CONV, P_RWKV + (j + 1) * D_CONV) for j in (1, 2)]
        for _ in pending:
            pass
        prev = shifted(ur, r0, r1)
        pending = _inproj_rows(ur, prev, conv_blocks, slice(r0, r1), params, outs)
    for _ in pending:
        pass
    carry[first:pad, :] = ur[sub - lag:sub]


def _inproj_rows(ur, prev, conv_blocks, rs, params, outs):
    mu_ref, w0_ref, a0_ref, wl_ref, kkp_ref, kap_ref, rkp_ref, bones_ref = params
    r_o, lw_o, km_o, v_o, kk_o, kka_o, bon_o, sgr_o, glu_o, sgc_o = outs
    glu_a, glu_b, g_c = conv_blocks
    urs = ur + (prev - ur) * mu_ref[...]
    r = urs[:, 0:D_RWKV]
    k = urs[:, D_RWKV:2 * D_RWKV]
    v = urs[:, 2 * D_RWKV:3 * D_RWKV]
    g_r = urs[:, 3 * D_RWKV:4 * D_RWKV]
    lora = urs[:, 4 * D_RWKV:P_RWKV]
    lane = lax.broadcasted_iota(jnp.int32, lora.shape, 1)
    lora_in = jnp.where(lane < LORA, jnp.tanh(lora), lora).astype(BF16)
    wa = _dg(lora_in, wl_ref[...])
    r_o[0, rs] = r
    v_o[0, rs] = v
    glu_o[0, rs] = glu_a * _sigmoid(glu_b)
    sgc_o[0, rs] = g_c * _sigmoid(g_c)
    yield

    z = -(w0_ref[...] + wa[:, :D_RWKV])
    softplus = jnp.maximum(z, 0.0) + jnp.log(1.0 + jnp.exp(-jnp.abs(z)))
    lw = -jnp.exp(-softplus - 0.5)
    a = _sigmoid(a0_ref[...] + wa[:, D_RWKV:])
    lw_o[0, rs] = lw
    bones = bones_ref[...]
    kkr = k * kkp_ref[...]
    ssq = _segsum(kkr * kkr, bones)
    yield

    kk = kkr * lax.rsqrt(jnp.maximum(ssq, 1e-24))
    kk_o[0, rs] = kk
    kka_o[0, rs] = kk * a
    km = k * (1.0 + (a - 1.0) * kap_ref[...])
    km_o[0, rs] = km
    rkk = _segsum(r * km * rkp_ref[...], bones)
    yield

    bon_o[0, rs] = rkk * v
    sgr_o[0, rs] = g_r * _sigmoid(g_r)


def _inproj_call(time_major, l, x, mod, s_shift, layer_params, bones, tile):
    if time_major:
        n_t, lag, _ = x.shape
        x = x.reshape(1, n_t * lag, D_MODEL)
        n_outer, n_inner, rows = 1, n_t * lag, n_t * lag
        grid = (1, 1)
        x_map = lambda i, t: (0, 0, 0)
        mod_spec = lambda j: pl.BlockSpec((None, lag, D_MODEL), lambda i, t: (l, 0, j))
        shift_spec = _layer_spec(s_shift, l, 2)
        hl_shape = (lag, D_MODEL)
        hl_spec = pl.BlockSpec(hl_shape, lambda i, t: (0, 0))
    else:
        n_outer, n_inner, _ = x.shape
        lag = 1
        rows = tile
        grid = (n_outer, n_inner // rows)
        x_map = lambda b, t: (b, t, 0)
        mod_spec = lambda j: pl.BlockSpec((None, None, 1, D_MODEL), lambda b, t: (l, b, 0, j))
        shift_spec = pl.BlockSpec((None, 1, D_MODEL), lambda b, t: (b, 0, 0))
        hl_shape = (n_outer, 1, D_MODEL)
        hl_spec = pl.BlockSpec((None, 1, D_MODEL), lambda b, t: (b, 0, 0))
    carry = pltpu.VMEM((-(-lag // SUBLANES) * SUBLANES + rows, P_RWKV), F32)
    out_block = pl.BlockSpec((1, rows, D_RWKV), x_map)
    out_sds = jax.ShapeDtypeStruct((n_outer, n_inner, D_RWKV), F32)
    return pl.pallas_call(
        functools.partial(_inproj_body, lag),
        grid=grid,
        in_specs=[pl.BlockSpec((1, rows, D_MODEL), x_map), mod_spec(0), mod_spec(1), shift_spec]
        + [_layer_spec(p, l, 2) for p in layer_params]
        + [pl.BlockSpec(bones.shape, lambda i, t: (0, 0))],
        out_specs=[out_block] * 10 + [hl_spec],
        out_shape=[out_sds] * 10 + [jax.ShapeDtypeStruct(hl_shape, F32)],
        scratch_shapes=[carry],
        compiler_params=pltpu.CompilerParams(
            dimension_semantics=("arbitrary", "arbitrary"), vmem_limit_bytes=VMEM_LIMIT),
        name="inproj_tm" if time_major else "inproj_bm",
    )(x, mod, mod, s_shift, *layer_params, bones)


PASS_GRAM = 1
PASS_INV = 1
PASS_STATE = 1
PASS_OUT = 1
CHUNK_SEQS = 8


def _wkv_chunk_body(r_ref, lw_ref, km_ref, v_ref, kk_ref, kka_ref, s0_ref, lt_ref,
                    y_ref, so_ref, s_scr):
    c = pl.program_id(1)
    n_seq = r_ref.shape[0]
    zeros_hh = jnp.zeros((HEAD_DIM, HEAD_DIM), F32)

    @pl.when(c == 0)
    def _():
        for i in range(n_seq):
            for p in range(N_PAIRS):
                top = jnp.concatenate([s0_ref[i, 2 * p], zeros_hh], axis=1)
                bot = jnp.concatenate([zeros_hh, s0_ref[i, 2 * p + 1]], axis=1)
                s_scr[i * N_PAIRS + p] = jnp.concatenate([top, bot], axis=0)

    lt = lt_ref[...]
    lane = lax.broadcasted_iota(jnp.int32, (CHUNK, LANES), 1)
    first = lane < HEAD_DIM
    row = lax.broadcasted_iota(jnp.int32, (LANES, LANES), 0)
    col = lax.broadcasted_iota(jnp.int32, (LANES, LANES), 1)
    same_head = (row < HEAD_DIM) == (col < HEAD_DIM)
    rt_i = row & (HEAD_DIM - 1)
    ct_i = col & (HEAD_DIM - 1)
    tri_strict = same_head & (ct_i < rt_i)
    tri_incl = same_head & (ct_i <= rt_i)
    eye = (row == col).astype(F32)

    def stack(xp):
        return jnp.concatenate([jnp.where(first, xp, 0.0), jnp.where(first, 0.0, xp)], axis=0)

    x2, y2, vs, hk, p_last = [], [], [], [], []
    for i in range(n_seq):
        lw = lw_ref[i]
        l_hi = lw.astype(BF16)
        l_r1 = lw - l_hi.astype(F32)
        l_mid = l_r1.astype(BF16)
        l_lo = (l_r1 - l_mid.astype(F32)).astype(BF16)
        cum = _dg(lt, l_hi) + (_dg(lt, l_mid) + _dg(lt, l_lo))
        cum_last = cum[CHUNK - 1:CHUNK, :]
        p_inv = jnp.exp(-cum)
        p_end = jnp.exp(cum_last - cum)
        at_all = -kk_ref[i] * jnp.exp(cum - lw)
        rt_all = r_ref[i] * jnp.exp(cum)
        kka = kka_ref[i]
        km = km_ref[i]
        bt_all = kka * p_inv
        kt_all = km * p_inv
        bh_all = kka * p_end
        kh_all = km * p_end
        v_all = v_ref[i]
        pl_all = jnp.exp(cum_last)
        for p in range(N_PAIRS):
            s = slice(LANES * p, LANES * (p + 1))
            x2.append(jnp.concatenate([stack(at_all[:, s]), stack(rt_all[:, s])], axis=0))
            y2.append(jnp.concatenate([stack(bt_all[:, s]), stack(kt_all[:, s])], axis=0))
            vs.append(stack(v_all[:, s]))
            hk.append(jnp.concatenate([stack(bh_all[:, s]), stack(kh_all[:, s])], axis=0))
            p_last.append(pl_all[:, s])

    units = range(n_seq * N_PAIRS)
    g = [_mm(x2[u], y2[u], _NT, PASS_GRAM) for u in units]
    n_pow = [jnp.where(tri_strict, g[u][:LANES, :LANES], 0.0) for u in units]
    a_ak = [jnp.where(tri_strict, g[u][:LANES, LANES:], 0.0) for u in units]
    a_r = [jnp.concatenate([jnp.where(tri_incl, g[u][LANES:, :LANES], 0.0),
                            jnp.where(tri_incl, g[u][LANES:, LANES:], 0.0)], axis=1) for u in units]

    t_inv = [eye + n_pow[u] for u in units]
    span = 2
    while span < CHUNK:
        n_pow = [_mm(n_pow[u], n_pow[u], _NN, PASS_INV) for u in units]
        t_inv = [t_inv[u] + _mm(t_inv[u], n_pow[u], _NN, PASS_INV) for u in units]
        span *= 2

    s_bd = [s_scr[u] for u in units]
    xs = [_mm(x2[u], s_bd[u], _NT, PASS_STATE) for u in units]
    akv = [_mm(a_ak[u], vs[u], _NN, PASS_STATE) for u in units]
    z_mat = [_mm(t_inv[u], xs[u][:LANES] + akv[u], _NN, PASS_STATE) for u in units]
    zv = [jnp.concatenate([z_mat[u], vs[u]], axis=0) for u in units]
    y_bd = [xs[u][LANES:] + _mm(a_r[u], zv[u], _NN, PASS_OUT) for u in units]
    s_new = [s_bd[u] * p_last[u] + _mm(zv[u], hk[u], _TN, PASS_STATE) for u in units]
    for u in units:
        i, p = divmod(u, N_PAIRS)
        y_ref[i, :, LANES * p:LANES * (p + 1)] = y_bd[u][:HEAD_DIM] + y_bd[u][HEAD_DIM:]
        s_scr[u] = s_new[u]
        so_ref[i, 2 * p] = s_new[u][:HEAD_DIM, :HEAD_DIM]
        so_ref[i, 2 * p + 1] = s_new[u][HEAD_DIM:, HEAD_DIM:]


def _wkv_chunk_call(r, lw, km, v, kk, kka, s0, lt):
    n_b, n_t, _ = r.shape
    nq = CHUNK_SEQS
    seq_block = pl.BlockSpec((nq, CHUNK, D_RWKV), lambda b, c: (b, c, 0))
    st_block = pl.BlockSpec((nq, N_HEADS, HEAD_DIM, HEAD_DIM), lambda b, c: (b, 0, 0, 0))
    return pl.pallas_call(
        _wkv_chunk_body,
        grid=(n_b // nq, n_t // CHUNK),
        in_specs=[seq_block] * 6 + [st_block, pl.BlockSpec(lt.shape, lambda b, c: (0, 0))],
        out_specs=[seq_block, st_block],
        out_shape=[jax.ShapeDtypeStruct(r.shape, F32), jax.ShapeDtypeStruct(s0.shape, F32)],
        scratch_shapes=[pltpu.VMEM((nq * N_PAIRS, LANES, LANES), F32)],
        compiler_params=pltpu.CompilerParams(
            dimension_semantics=("arbitrary", "arbitrary"), vmem_limit_bytes=VMEM_LIMIT),
        name="wkv_chunk",
    )(r, lw, km, v, kk, kka, s0, lt)


SHORT_T = 4
SHORT_SEQS = LANES // (2 * SHORT_T)


def _wkv_short_body(layer, r_ref, lw_ref, km_ref, v_ref, kk_ref, kka_ref, s_all_ref, *rest):
    y_ref, so_all_ref = rest[-2:]
    if layer is None:
        s_ref, so_ref = s_all_ref, so_all_ref
    else:
        s_ref, so_ref = s_all_ref.at[layer], so_all_ref.at[layer]
        for other in range(s_all_ref.shape[0]):
            if other != layer:
                so_all_ref[other] = s_all_ref[other]
    n_seq = s_ref.shape[0]
    grp = 2 * SHORT_T
    row = lax.broadcasted_iota(jnp.int32, (LANES, LANES), 0)
    col = lax.broadcasted_iota(jnp.int32, (LANES, LANES), 1)
    same_blk = (row // SHORT_T) == (col // SHORT_T)
    t_row = row % SHORT_T
    t_col = col % SHORT_T
    strict = same_blk & (t_col < t_row)
    incl = same_blk & (t_col <= t_row)
    eye = (row == col).astype(F32)
    sum_mat = jnp.concatenate([incl.astype(BF16), same_blk.astype(BF16)], axis=0)
    own = ((row // SHORT_T) % 2) == (col // HEAD_DIM)
    zeros_hh = jnp.zeros((HEAD_DIM, HEAD_DIM), F32)

    def rows_of(ref, p):
        return ref[:, :, LANES * p:LANES * (p + 1)].reshape(n_seq * grp, LANES)

    def mask(x):
        return jnp.where(own, x, 0.0)

    pairs = range(N_PAIRS)
    x2, y2, vs, bh, kh, p_last = [], [], [], [], [], []
    for p in pairs:
        lw = rows_of(lw_ref, p)
        l_hi = lw.astype(BF16)
        l_r1 = lw - l_hi.astype(F32)
        l_mid = l_r1.astype(BF16)
        l_lo = (l_r1 - l_mid.astype(F32)).astype(BF16)
        sums = _dg(sum_mat, l_hi) + (_dg(sum_mat, l_mid) + _dg(sum_mat, l_lo))
        cum = sums[:LANES]
        tot = sums[LANES:]
        p_inv = jnp.exp(-cum)
        p_end = jnp.exp(tot - cum)
        kka = rows_of(kka_ref, p)
        km = rows_of(km_ref, p)
        x2.append(jnp.concatenate([mask(-rows_of(kk_ref, p) * jnp.exp(cum - lw)),
                                   mask(rows_of(r_ref, p) * jnp.exp(cum))], axis=0))
        y2.append(jnp.concatenate([mask(kka * p_inv), mask(km * p_inv)], axis=0))
        vs.append(mask(rows_of(v_ref, p)))
        bh.append(mask(kka * p_end))
        kh.append(mask(km * p_end))
        p_last.append(jnp.exp(tot))

    g = [_mm(x2[p], y2[p], _NT, PASS_GRAM) for p in pairs]
    n_mat = [jnp.where(strict, g[p][:LANES, :LANES], 0.0) for p in pairs]
    a_ak = [jnp.where(strict, g[p][:LANES, LANES:], 0.0) for p in pairs]
    a_r = [jnp.concatenate([jnp.where(incl, g[p][LANES:, :LANES], 0.0),
                            jnp.where(incl, g[p][LANES:, LANES:], 0.0)], axis=1) for p in pairs]
    n_sq = [_mm(n_mat[p], n_mat[p], _NN, PASS_INV) for p in pairs]
    t_inv = [(eye + n_mat[p]) + _mm(eye + n_mat[p], n_sq[p], _NN, PASS_INV) for p in pairs]
    akv = [_mm(a_ak[p], vs[p], _NN, PASS_STATE) for p in pairs]

    def seq_rows(x, i):
        return x[grp * i:grp * (i + 1)]

    s_bd, xa_s, xr_s = [], [], []
    for p in pairs:
        s_p, xa_p, xr_p = [], [], []
        for i in range(n_seq):
            top = jnp.concatenate([s_ref[i, 2 * p], zeros_hh], axis=1)
            bot = jnp.concatenate([zeros_hh, s_ref[i, 2 * p + 1]], axis=1)
            s_i = jnp.concatenate([top, bot], axis=0)
            x_i = jnp.concatenate([seq_rows(x2[p][:LANES], i), seq_rows(x2[p][LANES:], i)], axis=0)
            xs_i = _mm(x_i, s_i, _NT, PASS_STATE)
            s_p.append(s_i)
            xa_p.append(xs_i[:grp])
            xr_p.append(xs_i[grp:])
        s_bd.append(s_p)
        xa_s.append(jnp.concatenate(xa_p, axis=0))
        xr_s.append(jnp.concatenate(xr_p, axis=0))

    z_mat = [_mm(t_inv[p], xa_s[p] + akv[p], _NN, PASS_STATE) for p in pairs]
    zv = [jnp.concatenate([z_mat[p], vs[p]], axis=0) for p in pairs]
    y_bd = [xr_s[p] + _mm(a_r[p], zv[p], _NN, PASS_OUT) for p in pairs]
    for p in pairs:
        y3 = y_bd[p].reshape(n_seq, grp, LANES)
        y_ref[:, :, LANES * p:LANES * (p + 1)] = y3 + jnp.concatenate(
            [y3[:, SHORT_T:], y3[:, :SHORT_T]], axis=1)
        for i in range(n_seq):
            zv_i = jnp.concatenate([seq_rows(z_mat[p], i), seq_rows(vs[p], i)], axis=0)
            hk_i = jnp.concatenate([seq_rows(bh[p], i), seq_rows(kh[p], i)], axis=0)
            s_new = (s_bd[p][i] * p_last[p][grp * i:grp * i + 1]
                     + _mm(zv_i, hk_i, _TN, PASS_STATE))
            so_ref[i, 2 * p] = s_new[:HEAD_DIM, :HEAD_DIM]
            so_ref[i, 2 * p + 1] = s_new[HEAD_DIM:, HEAD_DIM:]


def _wkv_short_call(l, r, lw, km, v, kk, kka, s_all, s_new_all):
    n_b = r.shape[0]
    nq = SHORT_SEQS
    seq_block = pl.BlockSpec((nq, 2 * SHORT_T, D_RWKV), lambda i: (i, 0, 0))
    carried = [] if s_new_all is None else [s_new_all]
    if carried:
        st_block = pl.BlockSpec((None, nq, N_HEADS, HEAD_DIM, HEAD_DIM), lambda i: (l, i, 0, 0, 0))
    else:
        st_block = pl.BlockSpec((s_all.shape[0], nq, N_HEADS, HEAD_DIM, HEAD_DIM),
                                lambda i: (0, i, 0, 0, 0))
    return pl.pallas_call(
        functools.partial(_wkv_short_body, None if carried else l),
        grid=(n_b // nq,),
        in_specs=[seq_block] * 6 + [st_block] + [pl.BlockSpec(memory_space=pl.ANY)] * len(carried),
        out_specs=[seq_block, st_block],
        out_shape=[jax.ShapeDtypeStruct(r.shape, F32), jax.ShapeDtypeStruct(s_all.shape, F32)],
        input_output_aliases={7: 1} if carried else {},
        compiler_params=pltpu.CompilerParams(
            dimension_semantics=("arbitrary",), vmem_limit_bytes=VMEM_LIMIT),
        name="wkv_short",
    )(r, lw, km, v, kk, kka, s_all, *carried)


CONV_ROW_BLOCK = 128
HIST_PAD = 32


def _merge_and_residual(y, bonus, sgr, conv, sgc, x, gate, gnr_g, gnr_b, gnc_g, gnc_b,
                        wout_ref, gpost, bones):
    y_r = (_group_norm(y, gnr_g, gnr_b, RWKV_GN_EPS, bones) + bonus) * sgr
    cn = _group_norm(conv, gnc_g, gnc_b, CONV_GN_EPS, bones)
    y_c = cn * _sigmoid(cn) * sgc
    mix = _dg(y_r.astype(BF16), wout_ref[0:D_RWKV, :]) + _dg(y_c.astype(BF16), wout_ref[D_RWKV:, :])
    ms = jnp.mean(mix * mix, axis=-1, keepdims=True)
    return x + gate * (mix * lax.rsqrt(ms + RMS_EPS) * gpost)


def _post_bm_body(y_ref, bon_ref, sgr_ref, glu_ref, sgc_ref, x_ref, gate_ref, sconv_ref,
                  gnrg_ref, gnrb_ref, wdw_ref, bdw_ref, gncg_ref, gncb_ref, wout_ref, gpost_ref,
                  bones_ref, xo_ref, buf, conv):
    step = pl.program_id(1)
    rows = y_ref.shape[1]
    off = HIST_PAD - HIST
    shifted_rows = rows + HIST_PAD - SUBLANES

    @pl.when(step == 0)
    def _():
        buf[0, 0:off, :] = jnp.zeros((off, D_CONV), F32)
        buf[0, off:HIST_PAD, :] = sconv_ref[0]

    buf[0, HIST_PAD:HIST_PAD + rows, :] = glu_ref[0]
    for q in range(1, SUBLANES):
        buf[q, 0:shifted_rows, :] = buf[0, q:q + shifted_rows, :]
    for r0 in range(0, rows, CONV_ROW_BLOCK):
        acc = jnp.broadcast_to(bdw_ref[...], (CONV_ROW_BLOCK, D_CONV))
        for j in range(CONV_W):
            q = (j + off) % SUBLANES
            base = r0 + j + off - q
            acc = acc + wdw_ref[j:j + 1, :] * buf[q, base:base + CONV_ROW_BLOCK, :]
        conv[r0:r0 + CONV_ROW_BLOCK, :] = acc
    buf[0, off:HIST_PAD, :] = buf[0, rows + off:rows + HIST_PAD, :]

    xo_ref[0] = _merge_and_residual(
        y_ref[0], bon_ref[0], sgr_ref[0], conv[...], sgc_ref[0], x_ref[0], gate_ref[...],
        gnrg_ref[...], gnrb_ref[...], gncg_ref[...], gncb_ref[...], wout_ref, gpost_ref[...],
        bones_ref[...])


def _post_tm_body(y_ref, bon_ref, sgr_ref, glu_ref, sgc_ref, x_ref, gate_ref, sconv_ref,
                  gnrg_ref, gnrb_ref, wdw_ref, bdw_ref, gncg_ref, gncb_ref, wout_ref, gpost_ref,
                  bones_ref, xo_ref, co_ref):
    n_t, rows, _ = y_ref.shape
    for i in range(n_t, n_t + HIST):
        co_ref[i - n_t] = sconv_ref[i] if i < HIST else glu_ref[i - HIST]
    convs = []
    for t in range(n_t):
        acc = jnp.broadcast_to(bdw_ref[...], (rows, D_CONV))
        for j in range(CONV_W):
            i = t + j
            src = sconv_ref[i] if i < HIST else glu_ref[i - HIST]
            acc = acc + wdw_ref[j:j + 1, :] * src
        convs.append(acc)
    flat = lambda ref: ref[...].reshape(n_t * rows, ref.shape[-1])
    out = _merge_and_residual(
        flat(y_ref), flat(bon_ref), flat(sgr_ref), jnp.concatenate(convs, axis=0), flat(sgc_ref),
        flat(x_ref), jnp.concatenate([gate_ref[...]] * n_t, axis=0),
        gnrg_ref[...], gnrb_ref[...], gncg_ref[...], gncb_ref[...], wout_ref, gpost_ref[...],
        bones_ref[...])
    xo_ref[...] = out.reshape(n_t, rows, D_MODEL)


def _post_call(time_major, l, y, bonus, sgr, glu, sgc, x, mod, s_conv, layer_params, bones, tile):
    n_outer, n_inner, _ = x.shape
    if time_major:
        rows = n_inner
        grid = (1, 1)
        lead = n_outer
        x_map = lambda i, t: (0, 0, 0)
        gate_spec = pl.BlockSpec((None, rows, D_MODEL), lambda i, t: (l, 0, 2))
        sconv_spec = _layer_spec(s_conv, l, 2)
        scratch = []
        body = _post_tm_body
    else:
        rows = tile
        grid = (n_outer, n_inner // rows)
        lead = 1
        x_map = lambda b, t: (b, t, 0)
        gate_spec = pl.BlockSpec((None, None, 1, D_MODEL), lambda b, t: (l, b, 0, 2))
        sconv_spec = pl.BlockSpec((1, HIST, D_CONV), lambda b, t: (b, 0, 0))
        scratch = [pltpu.VMEM((SUBLANES, HIST_PAD + rows, D_CONV), F32),
                   pltpu.VMEM((rows, D_CONV), F32)]
        body = _post_bm_body
    half = pl.BlockSpec((lead, rows, D_RWKV), x_map)
    full = pl.BlockSpec((lead, rows, D_MODEL), x_map)
    out_specs, out_shape = full, jax.ShapeDtypeStruct(x.shape, F32)
    if time_major:
        out_specs = [full, pl.BlockSpec(s_conv.shape[1:], lambda i, t: (0, 0, 0))]
        out_shape = [out_shape, jax.ShapeDtypeStruct(s_conv.shape[1:], F32)]
    return pl.pallas_call(
        body,
        grid=grid,
        in_specs=[half] * 5 + [full, gate_spec, sconv_spec]
        + [_layer_spec(p, l, 2) for p in layer_params]
        + [pl.BlockSpec(bones.shape, lambda i, t: (0, 0))],
        out_specs=out_specs,
        out_shape=out_shape,
        scratch_shapes=scratch,
        compiler_params=pltpu.CompilerParams(
            dimension_semantics=("arbitrary", "arbitrary"), vmem_limit_bytes=VMEM_LIMIT),
        name="post_tm" if time_major else "post_bm",
    )(y, bonus, sgr, glu, sgc, x, mod, s_conv, *layer_params, bones)


POST_TILE = 512


def _block_ones():
    i = lax.broadcasted_iota(jnp.int32, (MXU_DIM, MXU_DIM), 0) // HEAD_DIM
    j = lax.broadcasted_iota(jnp.int32, (MXU_DIM, MXU_DIM), 1) // HEAD_DIM
    return (i == j).astype(BF16)


def _lower_tri():
    i = lax.broadcasted_iota(jnp.int32, (CHUNK, CHUNK), 0)
    j = lax.broadcasted_iota(jnp.int32, (CHUNK, CHUNK), 1)
    return (j <= i).astype(BF16)


def kernel(x_prompt, x_sample, c_prompt, c_sample, state_shift, state_wkv, state_conv, w_ada, b_ada, g_pre, g_post, w_in, mu, w0, w_up, a0, a_up, k_k, k_a, r_k, gn_r_g, gn_r_b, w_dw, b_dw, gn_c_g, gn_c_b, w_out):
    depth = w_in.shape[0]
    n_p, t_p, _ = x_prompt.shape
    n_s, t_s, _ = x_sample.shape
    bones = _block_ones()
    lt = _lower_tri()

    mod_p, mod_s = _mod_call(c_prompt, c_sample, w_ada, b_ada)
    mod_p = mod_p[:, :, None, :]

    xp = x_prompt
    xs = jnp.transpose(x_sample, (1, 0, 2))
    zeros_shift = jnp.zeros((n_p, 1, D_MODEL), F32)
    zeros_wkv = jnp.zeros((n_p, N_HEADS, HEAD_DIM, HEAD_DIM), F32)
    zeros_conv = jnp.zeros((n_p, HIST, D_CONV), F32)
    sconv_tm = jnp.transpose(state_conv, (0, 2, 1, 3))

    def dup_rows(a):
        a = jnp.transpose(a, (1, 0, 2))
        return jnp.concatenate([a, a], axis=1)

    rows = lambda a: a.reshape(depth, 1, -1)
    zl = jnp.zeros((depth, LORA, D_RWKV), F32)
    w_lora = jnp.concatenate(
        [jnp.concatenate([w_up, zl], axis=2), jnp.concatenate([zl, a_up], axis=2)],
        axis=1).astype(BF16)
    in_params = (rows(g_pre), w_in.astype(BF16), rows(mu), rows(w0), rows(a0), w_lora,
                 rows(k_k), rows(k_a), rows(r_k))
    post_params = (rows(gn_r_g), rows(gn_r_b), w_dw, rows(b_dw), rows(gn_c_g), rows(gn_c_b),
                   w_out.astype(BF16), rows(g_post))

    shifts_p, wkvs_p, convs_p, shifts_s, convs_s = [], [], [], [], []
    wkv_s = None
    for l in range(depth):
        (r, lw, km, v, kk, kka, bonus, sgr, glu, sgc, hl) = _inproj_call(
            False, l, xp, mod_p, zeros_shift, in_params, bones, INPROJ_TILE)
        y, s_new = _wkv_chunk_call(r, lw, km, v, kk, kka, zeros_wkv, lt)
        xp = _post_call(False, l, y, bonus, sgr, glu, sgc, xp, mod_p, zeros_conv, post_params,
                        bones, POST_TILE)
        shifts_p.append(hl[:, 0])
        wkvs_p.append(s_new)
        convs_p.append(glu[:, -HIST:] if t_p >= HIST
                       else jnp.concatenate([zeros_conv, glu], axis=1)[:, -HIST:])

        *seq_outs, hl = _inproj_call(True, l, xs, mod_s, state_shift, in_params, bones, None)
        (r, lw, km, v, kk, kka, bonus, sgr, glu, sgc) = (
            a.reshape(t_s, n_s, D_RWKV) for a in seq_outs)
        y, wkv_s = _wkv_short_call(l, *(dup_rows(a) for a in (r, lw, km, v, kk, kka)),
                                   state_wkv, wkv_s)
        y = jnp.transpose(y[:, :SHORT_T], (1, 0, 2))
        xs, conv_new = _post_call(True, l, y, bonus, sgr, glu, sgc, xs, mod_s, sconv_tm,
                                  post_params, bones, None)
        shifts_s.append(hl)
        convs_s.append(conv_new)

    return (xp, jnp.transpose(xs, (1, 0, 2)),
            jnp.stack(shifts_p), jnp.stack(wkvs_p), jnp.stack(convs_p),
            jnp.stack(shifts_s), wkv_s, jnp.transpose(jnp.stack(convs_s), (0, 2, 1, 3)))
```

```python
import functools

import jax
import jax.numpy as jnp
from jax import lax
from jax.experimental import pallas as pl
from jax.experimental.pallas import tpu as pltpu

F32 = jnp.float32
BF16 = jnp.bfloat16

D_MODEL = 1024
D_RWKV = 512
D_CONV = 512
HEAD_DIM = 64
N_HEADS = 8
N_PAIRS = N_HEADS // 2
LORA = 64
P_RWKV = 4 * D_RWKV + 2 * LORA
P_IN = P_RWKV + 3 * D_CONV
CONV_W = 31
HIST = CONV_W - 1
RMS_EPS = 1e-6
RWKV_GN_EPS = 64e-5
CONV_GN_EPS = 1e-5

LANES = 128
SUBLANES = 8
MXU_DIM = 256
CHUNK = 64
VMEM_LIMIT = 56 * 1024 * 1024

_NN = (((1,), (0,)), ((), ()))
_NT = (((1,), (1,)), ((), ()))
_TN = (((0,), (0,)), ((), ()))


def _dg(a, b, dims=_NN):
    return lax.dot_general(a, b, dims, preferred_element_type=F32)


def _split2(x):
    hi = x.astype(BF16)
    lo = (x - hi.astype(F32)).astype(BF16)
    return hi, lo


def _mm(a, b, dims=_NN):
    return _dg(a.astype(BF16), b.astype(BF16), dims)


def _segsum(x, bones, split=False):
    pieces = _split2(x) if split else (x.astype(BF16),)
    cols = [slice(c0, c0 + MXU_DIM) for c0 in range(0, x.shape[1], MXU_DIM)]
    block = lambda s: functools.reduce(jnp.add, [_dg(p[:, s], bones) for p in pieces])
    return jnp.concatenate([block(s) for s in cols], axis=1)


def _sigmoid(x):
    return 1.0 / (1.0 + jnp.exp(-x))


def _group_norm(x, g, b, eps, bones):
    inv_n = 1.0 / HEAD_DIM
    mean = _segsum(x, bones, split=True) * inv_n
    d = x - mean
    var = _segsum(d * d, bones) * inv_n
    return d * lax.rsqrt(var + eps) * g + b


def _mod_body(cp_ref, cs_ref, w_ref, b_ref, op_ref, os_ref):
    w = w_ref[...].astype(BF16)
    for c_ref, o_ref in ((cp_ref, op_ref), (cs_ref, os_ref)):
        c = c_ref[...]
        o_ref[...] = _dg((c * _sigmoid(c)).astype(BF16), w) + b_ref[...]


def _mod_call(c_prompt, c_sample, w_ada, b_ada):
    depth, _, n3 = w_ada.shape
    bn = 768
    layer_cols = lambda l, j: (l, 0, j)
    whole = lambda a: pl.BlockSpec(a.shape, lambda l, j: (0, 0))
    out = lambda a: pl.BlockSpec((None, a.shape[0], bn), layer_cols)
    return pl.pallas_call(
        _mod_body,
        grid=(depth, n3 // bn),
        in_specs=[whole(c_prompt), whole(c_sample),
                  pl.BlockSpec((None, D_MODEL, bn), layer_cols),
                  pl.BlockSpec((None, 1, bn), layer_cols)],
        out_specs=[out(c_prompt), out(c_sample)],
        out_shape=[jax.ShapeDtypeStruct((depth, c.shape[0], n3), F32) for c in (c_prompt, c_sample)],
        compiler_params=pltpu.CompilerParams(
            dimension_semantics=("arbitrary", "arbitrary"), vmem_limit_bytes=VMEM_LIMIT),
        name="adaln_mod",
    )(c_prompt, c_sample, w_ada, b_ada.reshape(depth, 1, n3))


def _layer_spec(a, l, n_grid):
    idx = (l,) + (0,) * (a.ndim - 1)
    index_map = (lambda i: idx) if n_grid == 1 else (lambda i, t: idx)
    return pl.BlockSpec((None,) + a.shape[1:], index_map, pipeline_mode=pl.Buffered(1))


INPROJ_SUB = 256
INPROJ_TILE = 512


def _inproj_body(lag, x_ref, sh_ref, sc_ref, ss_ref, gpre_ref, win_ref, mu_ref,
                 w0_ref, a0_ref, wl_ref, kkp_ref, kap_ref, rkp_ref, bones_ref,
                 r_o, lw_o, km_o, v_o, kk_o, kka_o, bon_o, sgr_o, glu_o, sgc_o, hl_o, carry):
    step = pl.program_id(1)
    rows = x_ref.shape[1]
    sub = min(rows, INPROJ_SUB)
    bounds = [(r0, r0 + sub) for r0 in range(0, rows, sub)]
    pad = -(-lag // SUBLANES) * SUBLANES
    first = pad - lag

    @pl.when(step == 0)
    def _():
        ss = jnp.broadcast_to(ss_ref[...], (pad, D_MODEL)).astype(BF16)
        carry[first:pad, :] = _dg(ss, win_ref[:, :P_RWKV])[first:pad]

    def cols(hb, c0, c1):
        return _dg(hb, win_ref[:, c0:c1])

    def shifted(ur, r0, r1):
        carry[pad + r0:pad + r1, :] = ur
        return carry[first + r0:first + r1, :]

    def per_row(ref):
        v = ref[...]
        return v if lag == 1 else jnp.concatenate([v] * (sub // lag), axis=0)

    outs = (r_o, lw_o, km_o, v_o, kk_o, kka_o, bon_o, sgr_o, glu_o, sgc_o)
    params = (mu_ref, w0_ref, a0_ref, wl_ref, kkp_ref, kap_ref, rkp_ref, bones_ref)
    half = 2 * D_RWKV
    pending = iter(())
    for r0, r1 in bounds:
        x = x_ref[0, r0:r1]
        ms = jnp.mean(x * x, axis=-1, keepdims=True)
        h = x * lax.rsqrt(ms + RMS_EPS) * gpre_ref[...]
        h = h * (1.0 + per_row(sc_ref)) + per_row(sh_ref)
        if r1 == rows:
            hl_o[...] = h[sub - lag:sub]
        hb = h.astype(BF16)
        ur_a = cols(hb, 0, half)
        next(pending, None)
        ur = jnp.concatenate([ur_a, cols(hb, half, P_RWKV)], axis=1)
        next(pending, None)
        conv_blocks = [cols(hb, P_RWKV, P_RWKV + D_CONV)]
        next(pending, None)
        conv_blocks += [cols(hb, P_RWKV + j * D_CONV, P_RWKV + (j + 1) * D_CONV) for j in (1, 2)]
        for _ in pending:
            pass
        prev = shifted(ur, r0, r1)
        pending = _inproj_rows(ur, prev, conv_blocks, slice(r0, r1), params, outs)
    for _ in pending:
        pass
    carry[first:pad, :] = ur[sub - lag:sub]


def _inproj_rows(ur, prev, conv_blocks, rs, params, outs):
    mu_ref, w0_ref, a0_ref, wl_ref, kkp_ref, kap_ref, rkp_ref, bones_ref = params
    r_o, lw_o, km_o, v_o, kk_o, kka_o, bon_o, sgr_o, glu_o, sgc_o = outs
    glu_a, glu_b, g_c = conv_blocks
    urs = ur + (prev - ur) * mu_ref[...]
    r = urs[:, 0:D_RWKV]
    k = urs[:, D_RWKV:2 * D_RWKV]
    v = urs[:, 2 * D_RWKV:3 * D_RWKV]
    g_r = urs[:, 3 * D_RWKV:4 * D_RWKV]
    lora = urs[:, 4 * D_RWKV:P_RWKV]
    lane = lax.broadcasted_iota(jnp.int32, lora.shape, 1)
    lora_in = jnp.where(lane < LORA, jnp.tanh(lora), lora).astype(BF16)
    wa = _dg(lora_in, wl_ref[...])
    r_o[0, rs] = r
    v_o[0, rs] = v
    glu_o[0, rs] = glu_a * _sigmoid(glu_b)
    sgc_o[0, rs] = g_c * _sigmoid(g_c)
    yield

    z = -(w0_ref[...] + wa[:, :D_RWKV])
    softplus = jnp.maximum(z, 0.0) + jnp.log(1.0 + jnp.exp(-jnp.abs(z)))
    lw = -jnp.exp(-softplus - 0.5)
    a = _sigmoid(a0_ref[...] + wa[:, D_RWKV:])
    lw_o[0, rs] = lw
    bones = bones_ref[...]
    kkr = k * kkp_ref[...]
    ssq = _segsum(kkr * kkr, bones)
    yield

    kk = kkr * lax.rsqrt(jnp.maximum(ssq, 1e-24))
    kk_o[0, rs] = kk
    kka_o[0, rs] = kk * a
    km = k * (1.0 + (a - 1.0) * kap_ref[...])
    km_o[0, rs] = km
    rkk = _segsum(r * km * rkp_ref[...], bones)
    yield

    bon_o[0, rs] = rkk * v
    sgr_o[0, rs] = g_r * _sigmoid(g_r)


def _inproj_call(time_major, l, x, mod, s_shift, layer_params, bones, tile):
    if time_major:
        n_t, lag, _ = x.shape
        x = x.reshape(1, n_t * lag, D_MODEL)
        n_outer, n_inner, rows = 1, n_t * lag, n_t * lag
        grid = (1, 1)
        x_map = lambda i, t: (0, 0, 0)
        mod_spec = lambda j: pl.BlockSpec((None, lag, D_MODEL), lambda i, t: (l, 0, j))
        shift_spec = _layer_spec(s_shift, l, 2)
        hl_shape = (lag, D_MODEL)
        hl_spec = pl.BlockSpec(hl_shape, lambda i, t: (0, 0))
    else:
        n_outer, n_inner, _ = x.shape
        lag = 1
        rows = tile
        grid = (n_outer, n_inner // rows)
        x_map = lambda b, t: (b, t, 0)
        mod_spec = lambda j: pl.BlockSpec((None, None, 1, D_MODEL), lambda b, t: (l, b, 0, j))
        shift_spec = pl.BlockSpec((None, 1, D_MODEL), lambda b, t: (b, 0, 0))
        hl_shape = (n_outer, 1, D_MODEL)
        hl_spec = pl.BlockSpec((None, 1, D_MODEL), lambda b, t: (b, 0, 0))
    carry = pltpu.VMEM((-(-lag // SUBLANES) * SUBLANES + rows, P_RWKV), F32)
    out_block = pl.BlockSpec((1, rows, D_RWKV), x_map)
    out_sds = jax.ShapeDtypeStruct((n_outer, n_inner, D_RWKV), F32)
    return pl.pallas_call(
        functools.partial(_inproj_body, lag),
        grid=grid,
        in_specs=[pl.BlockSpec((1, rows, D_MODEL), x_map), mod_spec(0), mod_spec(1), shift_spec]
        + [_layer_spec(p, l, 2) for p in layer_params]
        + [pl.BlockSpec(bones.shape, lambda i, t: (0, 0))],
        out_specs=[out_block] * 10 + [hl_spec],
        out_shape=[out_sds] * 10 + [jax.ShapeDtypeStruct(hl_shape, F32)],
        scratch_shapes=[carry],
        compiler_params=pltpu.CompilerParams(
            dimension_semantics=("arbitrary", "arbitrary"), vmem_limit_bytes=VMEM_LIMIT),
        name="inproj_tm" if time_major else "inproj_bm",
    )(x, mod, mod, s_shift, *layer_params, bones)


CHUNK_SEQS = 8


def _wkv_chunk_body(r_ref, lw_ref, km_ref, v_ref, kk_ref, kka_ref, s0_ref, lt_ref,
                    y_ref, so_ref, s_scr):
    c = pl.program_id(1)
    n_seq = r_ref.shape[0]
    zeros_hh = jnp.zeros((HEAD_DIM, HEAD_DIM), F32)

    @pl.when(c == 0)
    def _():
        for i in range(n_seq):
            for p in range(N_PAIRS):
                top = jnp.concatenate([s0_ref[i, 2 * p], zeros_hh], axis=1)
                bot = jnp.concatenate([zeros_hh, s0_ref[i, 2 * p + 1]], axis=1)
                s_scr[i * N_PAIRS + p] = jnp.concatenate([top, bot], axis=0)

    lt = lt_ref[...]
    lane = lax.broadcasted_iota(jnp.int32, (CHUNK, LANES), 1)
    first = lane < HEAD_DIM
    row = lax.broadcasted_iota(jnp.int32, (LANES, LANES), 0)
    col = lax.broadcasted_iota(jnp.int32, (LANES, LANES), 1)
    same_head = (row < HEAD_DIM) == (col < HEAD_DIM)
    rt_i = row & (HEAD_DIM - 1)
    ct_i = col & (HEAD_DIM - 1)
    tri_strict = same_head & (ct_i < rt_i)
    tri_incl = same_head & (ct_i <= rt_i)
    eye = (row == col).astype(F32)

    def stack(xp):
        return jnp.concatenate([jnp.where(first, xp, 0.0), jnp.where(first, 0.0, xp)], axis=0)

    x2, y2, vs, hk, p_last = [], [], [], [], []
    for i in range(n_seq):
        lw = lw_ref[i]
        l_hi = lw.astype(BF16)
        l_r1 = lw - l_hi.astype(F32)
        l_mid = l_r1.astype(BF16)
        l_lo = (l_r1 - l_mid.astype(F32)).astype(BF16)
        cum = _dg(lt, l_hi) + (_dg(lt, l_mid) + _dg(lt, l_lo))
        cum_last = cum[CHUNK - 1:CHUNK, :]
        p_inv = jnp.exp(-cum)
        p_end = jnp.exp(cum_last - cum)
        at_all = -kk_ref[i] * jnp.exp(cum - lw)
        rt_all = r_ref[i] * jnp.exp(cum)
        kka = kka_ref[i]
        km = km_ref[i]
        bt_all = kka * p_inv
        kt_all = km * p_inv
        bh_all = kka * p_end
        kh_all = km * p_end
        v_all = v_ref[i]
        pl_all = jnp.exp(cum_last)
        for p in range(N_PAIRS):
            s = slice(LANES * p, LANES * (p + 1))
            x2.append(jnp.concatenate([stack(at_all[:, s]), stack(rt_all[:, s])], axis=0))
            y2.append(jnp.concatenate([stack(bt_all[:, s]), stack(kt_all[:, s])], axis=0))
            vs.append(stack(v_all[:, s]))
            hk.append(jnp.concatenate([stack(bh_all[:, s]), stack(kh_all[:, s])], axis=0))
            p_last.append(pl_all[:, s])

    units = range(n_seq * N_PAIRS)
    g = [_mm(x2[u], y2[u], _NT) for u in units]
    n_pow = [jnp.where(tri_strict, g[u][:LANES, :LANES], 0.0) for u in units]
    a_ak = [jnp.where(tri_strict, g[u][:LANES, LANES:], 0.0) for u in units]
    a_r = [jnp.concatenate([jnp.where(tri_incl, g[u][LANES:, :LANES], 0.0),
                            jnp.where(tri_incl, g[u][LANES:, LANES:], 0.0)], axis=1) for u in units]

    t_inv = [eye + n_pow[u] for u in units]
    span = 2
    while span < CHUNK:
        n_pow = [_mm(n_pow[u], n_pow[u], _NN) for u in units]
        t_inv = [t_inv[u] + _mm(t_inv[u], n_pow[u], _NN) for u in units]
        span *= 2

    s_bd = [s_scr[u] for u in units]
    xs = [_mm(x2[u], s_bd[u], _NT) for u in units]
    akv = [_mm(a_ak[u], vs[u], _NN) for u in units]
    z_mat = [_mm(t_inv[u], xs[u][:LANES] + akv[u], _NN) for u in units]
    zv = [jnp.concatenate([z_mat[u], vs[u]], axis=0) for u in units]
    y_bd = [xs[u][LANES:] + _mm(a_r[u], zv[u], _NN) for u in units]
    s_new = [s_bd[u] * p_last[u] + _mm(zv[u], hk[u], _TN) for u in units]
    for u in units:
        i, p = divmod(u, N_PAIRS)
        y_ref[i, :, LANES * p:LANES * (p + 1)] = y_bd[u][:HEAD_DIM] + y_bd[u][HEAD_DIM:]
        s_scr[u] = s_new[u]
        so_ref[i, 2 * p] = s_new[u][:HEAD_DIM, :HEAD_DIM]
        so_ref[i, 2 * p + 1] = s_new[u][HEAD_DIM:, HEAD_DIM:]


def _wkv_chunk_call(r, lw, km, v, kk, kka, s0, lt):
    n_b, n_t, _ = r.shape
    nq = CHUNK_SEQS
    seq_block = pl.BlockSpec((nq, CHUNK, D_RWKV), lambda b, c: (b, c, 0))
    st_block = pl.BlockSpec((nq, N_HEADS, HEAD_DIM, HEAD_DIM), lambda b, c: (b, 0, 0, 0))
    return pl.pallas_call(
        _wkv_chunk_body,
        grid=(n_b // nq, n_t // CHUNK),
        in_specs=[seq_block] * 6 + [st_block, pl.BlockSpec(lt.shape, lambda b, c: (0, 0))],
        out_specs=[seq_block, st_block],
        out_shape=[jax.ShapeDtypeStruct(r.shape, F32), jax.ShapeDtypeStruct(s0.shape, F32)],
        scratch_shapes=[pltpu.VMEM((nq * N_PAIRS, LANES, LANES), F32)],
        compiler_params=pltpu.CompilerParams(
            dimension_semantics=("arbitrary", "arbitrary"), vmem_limit_bytes=VMEM_LIMIT),
        name="wkv_chunk",
    )(r, lw, km, v, kk, kka, s0, lt)


SHORT_T = 4
SHORT_SEQS = LANES // (2 * SHORT_T)


def _wkv_short_body(layer, r_ref, lw_ref, km_ref, v_ref, kk_ref, kka_ref, s_all_ref, *rest):
    y_ref, so_all_ref = rest[-2:]
    if layer is None:
        s_ref, so_ref = s_all_ref, so_all_ref
    else:
        s_ref, so_ref = s_all_ref.at[layer], so_all_ref.at[layer]
        for other in range(s_all_ref.shape[0]):
            if other != layer:
                so_all_ref[other] = s_all_ref[other]
    n_seq = s_ref.shape[0]
    grp = 2 * SHORT_T
    row = lax.broadcasted_iota(jnp.int32, (LANES, LANES), 0)
    col = lax.broadcasted_iota(jnp.int32, (LANES, LANES), 1)
    same_blk = (row // SHORT_T) == (col // SHORT_T)
    t_row = row % SHORT_T
    t_col = col % SHORT_T
    strict = same_blk & (t_col < t_row)
    incl = same_blk & (t_col <= t_row)
    eye = (row == col).astype(F32)
    sum_mat = jnp.concatenate([incl.astype(BF16), same_blk.astype(BF16)], axis=0)
    own = ((row // SHORT_T) % 2) == (col // HEAD_DIM)
    zeros_hh = jnp.zeros((HEAD_DIM, HEAD_DIM), F32)

    def rows_of(ref, p):
        return ref[:, :, LANES * p:LANES * (p + 1)].reshape(n_seq * grp, LANES)

    def mask(x):
        return jnp.where(own, x, 0.0)

    pairs = range(N_PAIRS)
    x2, y2, vs, bh, kh, p_last = [], [], [], [], [], []
    for p in pairs:
        lw = rows_of(lw_ref, p)
        l_hi = lw.astype(BF16)
        l_r1 = lw - l_hi.astype(F32)
        l_mid = l_r1.astype(BF16)
        l_lo = (l_r1 - l_mid.astype(F32)).astype(BF16)
        sums = _dg(sum_mat, l_hi) + (_dg(sum_mat, l_mid) + _dg(sum_mat, l_lo))
        cum = sums[:LANES]
        tot = sums[LANES:]
        p_inv = jnp.exp(-cum)
        p_end = jnp.exp(tot - cum)
        kka = rows_of(kka_ref, p)
        km = rows_of(km_ref, p)
        x2.append(jnp.concatenate([mask(-rows_of(kk_ref, p) * jnp.exp(cum - lw)),
                                   mask(rows_of(r_ref, p) * jnp.exp(cum))], axis=0))
        y2.append(jnp.concatenate([mask(kka * p_inv), mask(km * p_inv)], axis=0))
        vs.append(mask(rows_of(v_ref, p)))
        bh.append(mask(kka * p_end))
        kh.append(mask(km * p_end))
        p_last.append(jnp.exp(tot))

    g = [_mm(x2[p], y2[p], _NT) for p in pairs]
    n_mat = [jnp.where(strict, g[p][:LANES, :LANES], 0.0) for p in pairs]
    a_ak = [jnp.where(strict, g[p][:LANES, LANES:], 0.0) for p in pairs]
    a_r = [jnp.concatenate([jnp.where(incl, g[p][LANES:, :LANES], 0.0),
                            jnp.where(incl, g[p][LANES:, LANES:], 0.0)], axis=1) for p in pairs]
    n_sq = [_mm(n_mat[p], n_mat[p], _NN) for p in pairs]
    t_inv = [(eye + n_mat[p]) + _mm(eye + n_mat[p], n_sq[p], _NN) for p in pairs]
    akv = [_mm(a_ak[p], vs[p], _NN) for p in pairs]

    def seq_rows(x, i):
        return x[grp * i:grp * (i + 1)]

    s_bd, xa_s, xr_s = [], [], []
    for p in pairs:
        s_p, xa_p, xr_p = [], [], []
        for i in range(n_seq):
            top = jnp.concatenate([s_ref[i, 2 * p], zeros_hh], axis=1)
            bot = jnp.concatenate([zeros_hh, s_ref[i, 2 * p + 1]], axis=1)
            s_i = jnp.concatenate([top, bot], axis=0)
            x_i = jnp.concatenate([seq_rows(x2[p][:LANES], i), seq_rows(x2[p][LANES:], i)], axis=0)
            xs_i = _mm(x_i, s_i, _NT)
            s_p.append(s_i)
            xa_p.append(xs_i[:grp])
            xr_p.append(xs_i[grp:])
        s_bd.append(s_p)
        xa_s.append(jnp.concatenate(xa_p, axis=0))
        xr_s.append(jnp.concatenate(xr_p, axis=0))

    z_mat = [_mm(t_inv[p], xa_s[p] + akv[p], _NN) for p in pairs]
    zv = [jnp.concatenate([z_mat[p], vs[p]], axis=0) for p in pairs]
    y_bd = [xr_s[p] + _mm(a_r[p], zv[p], _NN) for p in pairs]
    for p in pairs:
        y3 = y_bd[p].reshape(n_seq, grp, LANES)
        y_ref[:, :, LANES * p:LANES * (p + 1)] = y3 + jnp.concatenate(
            [y3[:, SHORT_T:], y3[:, :SHORT_T]], axis=1)
        for i in range(n_seq):
            zv_i = jnp.concatenate([seq_rows(z_mat[p], i), seq_rows(vs[p], i)], axis=0)
            hk_i = jnp.concatenate([seq_rows(bh[p], i), seq_rows(kh[p], i)], axis=0)
            s_new = (s_bd[p][i] * p_last[p][grp * i:grp * i + 1]
                     + _mm(zv_i, hk_i, _TN))
            so_ref[i, 2 * p] = s_new[:HEAD_DIM, :HEAD_DIM]
            so_ref[i, 2 * p + 1] = s_new[HEAD_DIM:, HEAD_DIM:]


def _wkv_short_call(l, r, lw, km, v, kk, kka, s_all, s_new_all):
    n_b = r.shape[0]
    nq = SHORT_SEQS
    seq_block = pl.BlockSpec((nq, 2 * SHORT_T, D_RWKV), lambda i: (i, 0, 0))
    carried = [] if s_new_all is None else [s_new_all]
    if carried:
        st_block = pl.BlockSpec((None, nq, N_HEADS, HEAD_DIM, HEAD_DIM), lambda i: (l, i, 0, 0, 0))
    else:
        st_block = pl.BlockSpec((s_all.shape[0], nq, N_HEADS, HEAD_DIM, HEAD_DIM),
                                lambda i: (0, i, 0, 0, 0))
    return pl.pallas_call(
        functools.partial(_wkv_short_body, None if carried else l),
        grid=(n_b // nq,),
        in_specs=[seq_block] * 6 + [st_block] + [pl.BlockSpec(memory_space=pl.ANY)] * len(carried),
        out_specs=[seq_block, st_block],
        out_shape=[jax.ShapeDtypeStruct(r.shape, F32), jax.ShapeDtypeStruct(s_all.shape, F32)],
        input_output_aliases={7: 1} if carried else {},
        compiler_params=pltpu.CompilerParams(
            dimension_semantics=("arbitrary",), vmem_limit_bytes=VMEM_LIMIT),
        name="wkv_short",
    )(r, lw, km, v, kk, kka, s_all, *carried)


CONV_ROW_BLOCK = 256
HIST_PAD = 32


def _merge_and_residual(y, bonus, sgr, conv, sgc, x, gate, gnr_g, gnr_b, gnc_g, gnc_b,
                        wout_ref, gpost, bones):
    y_r = (_group_norm(y, gnr_g, gnr_b, RWKV_GN_EPS, bones) + bonus) * sgr
    cn = _group_norm(conv, gnc_g, gnc_b, CONV_GN_EPS, bones)
    y_c = cn * _sigmoid(cn) * sgc
    mix = _dg(y_r.astype(BF16), wout_ref[0:D_RWKV, :]) + _dg(y_c.astype(BF16), wout_ref[D_RWKV:, :])
    ms = jnp.mean(mix * mix, axis=-1, keepdims=True)
    return x + gate * (mix * lax.rsqrt(ms + RMS_EPS) * gpost)


def _post_bm_body(y_ref, bon_ref, sgr_ref, glu_ref, sgc_ref, x_ref, gate_ref, sconv_ref,
                  gnrg_ref, gnrb_ref, wdw_ref, bdw_ref, gncg_ref, gncb_ref, wout_ref, gpost_ref,
                  bones_ref, xo_ref, buf, conv):
    step = pl.program_id(1)
    rows = y_ref.shape[1]
    off = HIST_PAD - HIST
    shifted_rows = rows + HIST_PAD - SUBLANES

    @pl.when(step == 0)
    def _():
        buf[0, 0:off, :] = jnp.zeros((off, D_CONV), F32)
        buf[0, off:HIST_PAD, :] = sconv_ref[0]

    buf[0, HIST_PAD:HIST_PAD + rows, :] = glu_ref[0]
    for q in range(1, SUBLANES):
        buf[q, 0:shifted_rows, :] = buf[0, q:q + shifted_rows, :]
    for r0 in range(0, rows, CONV_ROW_BLOCK):
        acc = jnp.broadcast_to(bdw_ref[...], (CONV_ROW_BLOCK, D_CONV))
        for j in range(CONV_W):
            q = (j + off) % SUBLANES
            base = r0 + j + off - q
            acc = acc + wdw_ref[j:j + 1, :] * buf[q, base:base + CONV_ROW_BLOCK, :]
        conv[r0:r0 + CONV_ROW_BLOCK, :] = acc
    buf[0, off:HIST_PAD, :] = buf[0, rows + off:rows + HIST_PAD, :]

    xo_ref[0] = _merge_and_residual(
        y_ref[0], bon_ref[0], sgr_ref[0], conv[...], sgc_ref[0], x_ref[0], gate_ref[...],
        gnrg_ref[...], gnrb_ref[...], gncg_ref[...], gncb_ref[...], wout_ref, gpost_ref[...],
        bones_ref[...])


def _post_tm_body(y_ref, bon_ref, sgr_ref, glu_ref, sgc_ref, x_ref, gate_ref, sconv_ref,
                  gnrg_ref, gnrb_ref, wdw_ref, bdw_ref, gncg_ref, gncb_ref, wout_ref, gpost_ref,
                  bones_ref, xo_ref, co_ref):
    n_t, rows, _ = y_ref.shape
    for i in range(n_t, n_t + HIST):
        co_ref[i - n_t] = sconv_ref[i] if i < HIST else glu_ref[i - HIST]
    convs = []
    for t in range(n_t):
        acc = jnp.broadcast_to(bdw_ref[...], (rows, D_CONV))
        for j in range(CONV_W):
            i = t + j
            src = sconv_ref[i] if i < HIST else glu_ref[i - HIST]
            acc = acc + wdw_ref[j:j + 1, :] * src
        convs.append(acc)
    flat = lambda ref: ref[...].reshape(n_t * rows, ref.shape[-1])
    out = _merge_and_residual(
        flat(y_ref), flat(bon_ref), flat(sgr_ref), jnp.concatenate(convs, axis=0), flat(sgc_ref),
        flat(x_ref), jnp.concatenate([gate_ref[...]] * n_t, axis=0),
        gnrg_ref[...], gnrb_ref[...], gncg_ref[...], gncb_ref[...], wout_ref, gpost_ref[...],
        bones_ref[...])
    xo_ref[...] = out.reshape(n_t, rows, D_MODEL)


def _post_call(time_major, l, y, bonus, sgr, glu, sgc, x, mod, s_conv, layer_params, bones, tile):
    n_outer, n_inner, _ = x.shape
    if time_major:
        rows = n_inner
        grid = (1, 1)
        lead = n_outer
        x_map = lambda i, t: (0, 0, 0)
        gate_spec = pl.BlockSpec((None, rows, D_MODEL), lambda i, t: (l, 0, 2))
        sconv_spec = _layer_spec(s_conv, l, 2)
        scratch = []
        body = _post_tm_body
    else:
        rows = tile
        grid = (n_outer, n_inner // rows)
        lead = 1
        x_map = lambda b, t: (b, t, 0)
        gate_spec = pl.BlockSpec((None, None, 1, D_MODEL), lambda b, t: (l, b, 0, 2))
        sconv_spec = pl.BlockSpec((1, HIST, D_CONV), lambda b, t: (b, 0, 0))
        scratch = [pltpu.VMEM((SUBLANES, HIST_PAD + rows, D_CONV), F32),
                   pltpu.VMEM((rows, D_CONV), F32)]
        body = _post_bm_body
    half = pl.BlockSpec((lead, rows, D_RWKV), x_map)
    full = pl.BlockSpec((lead, rows, D_MODEL), x_map)
    out_specs, out_shape = full, jax.ShapeDtypeStruct(x.shape, F32)
    if time_major:
        out_specs = [full, pl.BlockSpec(s_conv.shape[1:], lambda i, t: (0, 0, 0))]
        out_shape = [out_shape, jax.ShapeDtypeStruct(s_conv.shape[1:], F32)]
    return pl.pallas_call(
        body,
        grid=grid,
        in_specs=[half] * 5 + [full, gate_spec, sconv_spec]
        + [_layer_spec(p, l, 2) for p in layer_params]
        + [pl.BlockSpec(bones.shape, lambda i, t: (0, 0))],
        out_specs=out_specs,
        out_shape=out_shape,
        scratch_shapes=scratch,
        compiler_params=pltpu.CompilerParams(
            dimension_semantics=("arbitrary", "arbitrary"), vmem_limit_bytes=VMEM_LIMIT),
        name="post_tm" if time_major else "post_bm",
    )(y, bonus, sgr, glu, sgc, x, mod, s_conv, *layer_params, bones)


POST_TILE = 512


def _block_ones():
    i = lax.broadcasted_iota(jnp.int32, (MXU_DIM, MXU_DIM), 0) // HEAD_DIM
    j = lax.broadcasted_iota(jnp.int32, (MXU_DIM, MXU_DIM), 1) // HEAD_DIM
    return (i == j).astype(BF16)


def _lower_tri():
    i = lax.broadcasted_iota(jnp.int32, (CHUNK, CHUNK), 0)
    j = lax.broadcasted_iota(jnp.int32, (CHUNK, CHUNK), 1)
    return (j <= i).astype(BF16)


def kernel(x_prompt, x_sample, c_prompt, c_sample, state_shift, state_wkv, state_conv, w_ada, b_ada, g_pre, g_post, w_in, mu, w0, w_up, a0, a_up, k_k, k_a, r_k, gn_r_g, gn_r_b, w_dw, b_dw, gn_c_g, gn_c_b, w_out):
    depth = w_in.shape[0]
    n_p, t_p, _ = x_prompt.shape
    n_s, t_s, _ = x_sample.shape
    bones = _block_ones()
    lt = _lower_tri()

    mod_p, mod_s = _mod_call(c_prompt, c_sample, w_ada, b_ada)
    mod_p = mod_p[:, :, None, :]

    xp = x_prompt
    xs = jnp.transpose(x_sample, (1, 0, 2))
    zeros_shift = jnp.zeros((n_p, 1, D_MODEL), F32)
    zeros_wkv = jnp.zeros((n_p, N_HEADS, HEAD_DIM, HEAD_DIM), F32)
    zeros_conv = jnp.zeros((n_p, HIST, D_CONV), F32)
    sconv_tm = jnp.transpose(state_conv, (0, 2, 1, 3))

    def dup_rows(a):
        a = jnp.transpose(a, (1, 0, 2))
        return jnp.concatenate([a, a], axis=1)

    rows = lambda a: a.reshape(depth, 1, -1)
    zl = jnp.zeros((depth, LORA, D_RWKV), F32)
    w_lora = jnp.concatenate(
        [jnp.concatenate([w_up, zl], axis=2), jnp.concatenate([zl, a_up], axis=2)],
        axis=1).astype(BF16)
    in_params = (rows(g_pre), w_in.astype(BF16), rows(mu), rows(w0), rows(a0), w_lora,
                 rows(k_k), rows(k_a), rows(r_k))
    post_params = (rows(gn_r_g), rows(gn_r_b), w_dw, rows(b_dw), rows(gn_c_g), rows(gn_c_b),
                   w_out.astype(BF16), rows(g_post))

    shifts_p, wkvs_p, convs_p, shifts_s, convs_s = [], [], [], [], []
    wkv_s = None
    for l in range(depth):
        (r, lw, km, v, kk, kka, bonus, sgr, glu, sgc, hl) = _inproj_call(
            False, l, xp, mod_p, zeros_shift, in_params, bones, INPROJ_TILE)
        y, s_new = _wkv_chunk_call(r, lw, km, v, kk, kka, zeros_wkv, lt)
        xp = _post_call(False, l, y, bonus, sgr, glu, sgc, xp, mod_p, zeros_conv, post_params,
                        bones, POST_TILE)
        shifts_p.append(hl[:, 0])
        wkvs_p.append(s_new)
        convs_p.append(glu[:, -HIST:] if t_p >= HIST
                       else jnp.concatenate([zeros_conv, glu], axis=1)[:, -HIST:])

        *seq_outs, hl = _inproj_call(True, l, xs, mod_s, state_shift, in_params, bones, None)
        (r, lw, km, v, kk, kka, bonus, sgr, glu, sgc) = (
            a.reshape(t_s, n_s, D_RWKV) for a in seq_outs)
        y, wkv_s = _wkv_short_call(l, *(dup_rows(a) for a in (r, lw, km, v, kk, kka)),
                                   state_wkv, wkv_s)
        y = jnp.transpose(y[:, :SHORT_T], (1, 0, 2))
        xs, conv_new = _post_call(True, l, y, bonus, sgr, glu, sgc, xs, mod_s, sconv_tm,
                                  post_params, bones, None)
        shifts_s.append(hl)
        convs_s.append(conv_new)

    return (xp, jnp.transpose(xs, (1, 0, 2)),
            jnp.stack(shifts_p), jnp.stack(wkvs_p), jnp.stack(convs_p),
            jnp.stack(shifts_s), wkv_s, jnp.transpose(jnp.stack(convs_s), (0, 2, 1, 3)))
```

```python
import functools

import jax
import jax.numpy as jnp
from jax import lax
from jax.experimental import pallas as pl
from jax.experimental.pallas import tpu as pltpu

F32 = jnp.float32
BF16 = jnp.bfloat16

D_MODEL = 1024
D_RWKV = 512
D_CONV = 512
HEAD_DIM = 64
N_HEADS = 8
N_PAIRS = N_HEADS // 2
LORA = 64
P_RWKV = 4 * D_RWKV + 2 * LORA
P_IN = P_RWKV + 3 * D_CONV
CONV_W = 31
HIST = CONV_W - 1
RMS_EPS = 1e-6
RWKV_GN_EPS = 64e-5
CONV_GN_EPS = 1e-5

LANES = 128
SUBLANES = 8
MXU_DIM = 256
CHUNK = 64
VMEM_LIMIT = 56 * 1024 * 1024

_NN = (((1,), (0,)), ((), ()))
_NT = (((1,), (1,)), ((), ()))
_TN = (((0,), (0,)), ((), ()))


def _dg(a, b, dims=_NN):
    return lax.dot_general(a, b, dims, preferred_element_type=F32)


def _split2(x):
    hi = x.astype(BF16)
    lo = (x - hi.astype(F32)).astype(BF16)
    return hi, lo


def _mm(a, b, dims=_NN):
    return _dg(a.astype(BF16), b.astype(BF16), dims)


def _segsum(x, bones, split=False):
    pieces = _split2(x) if split else (x.astype(BF16),)
    cols = [slice(c0, c0 + MXU_DIM) for c0 in range(0, x.shape[1], MXU_DIM)]
    block = lambda s: functools.reduce(jnp.add, [_dg(p[:, s], bones) for p in pieces])
    return jnp.concatenate([block(s) for s in cols], axis=1)


def _sigmoid(x):
    return 1.0 / (1.0 + jnp.exp(-x))


def _group_norm(x, g, b, eps, bones):
    inv_n = 1.0 / HEAD_DIM
    mean = _segsum(x, bones, split=True) * inv_n
    d = x - mean
    var = _segsum(d * d, bones) * inv_n
    return d * lax.rsqrt(var + eps) * g + b


def _mod_body(cp_ref, cs_ref, w_ref, b_ref, op_ref, os_ref):
    w = w_ref[...].astype(BF16)
    for c_ref, o_ref in ((cp_ref, op_ref), (cs_ref, os_ref)):
        c = c_ref[...]
        o_ref[...] = _dg((c * _sigmoid(c)).astype(BF16), w) + b_ref[...]


def _mod_call(c_prompt, c_sample, w_ada, b_ada):
    depth, _, n3 = w_ada.shape
    bn = 768
    layer_cols = lambda l, j: (l, 0, j)
    whole = lambda a: pl.BlockSpec(a.shape, lambda l, j: (0, 0))
    out = lambda a: pl.BlockSpec((None, a.shape[0], bn), layer_cols)
    return pl.pallas_call(
        _mod_body,
        grid=(depth, n3 // bn),
        in_specs=[whole(c_prompt), whole(c_sample),
                  pl.BlockSpec((None, D_MODEL, bn), layer_cols),
                  pl.BlockSpec((None, 1, bn), layer_cols)],
        out_specs=[out(c_prompt), out(c_sample)],
        out_shape=[jax.ShapeDtypeStruct((depth, c.shape[0], n3), F32) for c in (c_prompt, c_sample)],
        compiler_params=pltpu.CompilerParams(
            dimension_semantics=("arbitrary", "arbitrary"), vmem_limit_bytes=VMEM_LIMIT),
        name="adaln_mod",
    )(c_prompt, c_sample, w_ada, b_ada.reshape(depth, 1, n3))


def _layer_spec(a, l, n_grid):
    idx = (l,) + (0,) * (a.ndim - 1)
    index_map = (lambda i: idx) if n_grid == 1 else (lambda i, t: idx)
    return pl.BlockSpec((None,) + a.shape[1:], index_map, pipeline_mode=pl.Buffered(1))


INPROJ_SUB = 256
INPROJ_TILE = 512


def _inproj_body(lag, x_ref, sh_ref, sc_ref, ss_ref, gpre_ref, win_ref, mu_ref,
                 w0_ref, a0_ref, wl_ref, kkp_ref, kap_ref, rkp_ref, bones_ref,
                 r_o, lw_o, km_o, v_o, kk_o, kka_o, bon_o, sgr_o, glu_o, sgc_o, hl_o, carry):
    step = pl.program_id(1)
    rows = x_ref.shape[1]
    sub = min(rows, INPROJ_SUB)
    bounds = [(r0, r0 + sub) for r0 in range(0, rows, sub)]
    pad = -(-lag // SUBLANES) * SUBLANES
    first = pad - lag

    @pl.when(step == 0)
    def _():
        ss = jnp.broadcast_to(ss_ref[...], (pad, D_MODEL)).astype(BF16)
        carry[first:pad, :] = _dg(ss, win_ref[:, :P_RWKV])[first:pad]

    def cols(hb, c0, c1):
        return _dg(hb, win_ref[:, c0:c1])

    def shifted(ur, r0, r1):
        carry[pad + r0:pad + r1, :] = ur
        return carry[first + r0:first + r1, :]

    def per_row(ref):
        v = ref[...]
        return v if lag == 1 else jnp.concatenate([v] * (sub // lag), axis=0)

    outs = (r_o, lw_o, km_o, v_o, kk_o, kka_o, bon_o, sgr_o, glu_o, sgc_o)
    params = (mu_ref, w0_ref, a0_ref, wl_ref, kkp_ref, kap_ref, rkp_ref, bones_ref)
    half = 2 * D_RWKV
    pending = iter(())
    for r0, r1 in bounds:
        x = x_ref[0, r0:r1]
        ms = jnp.mean(x * x, axis=-1, keepdims=True)
        h = x * lax.rsqrt(ms + RMS_EPS) * gpre_ref[...]
        h = h * (1.0 + per_row(sc_ref)) + per_row(sh_ref)
        if r1 == rows:
            hl_o[...] = h[sub - lag:sub]
        hb = h.astype(BF16)
        ur_a = cols(hb, 0, half)
        next(pending, None)
        ur = jnp.concatenate([ur_a, cols(hb, half, P_RWKV)], axis=1)
        next(pending, None)
        conv_blocks = [cols(hb, P_RWKV, P_RWKV + D_CONV)]
        next(pending, None)
        conv_blocks += [cols(hb, P_RWKV + j * D_CONV, P_RWKV + (j + 1) * D_CONV) for j in (1, 2)]
        for _ in pending:
            pass
        prev = shifted(ur, r0, r1)
        pending = _inproj_rows(ur, prev, conv_blocks, slice(r0, r1), params, outs)
    for _ in pending:
        pass
    carry[first:pad, :] = ur[sub - lag:sub]


def _inproj_rows(ur, prev, conv_blocks, rs, params, outs):
    mu_ref, w0_ref, a0_ref, wl_ref, kkp_ref, kap_ref, rkp_ref, bones_ref = params
    r_o, lw_o, km_o, v_o, kk_o, kka_o, bon_o, sgr_o, glu_o, sgc_o = outs
    glu_a, glu_b, g_c = conv_blocks
    urs = ur + (prev - ur) * mu_ref[...]
    r = urs[:, 0:D_RWKV]
    k = urs[:, D_RWKV:2 * D_RWKV]
    v = urs[:, 2 * D_RWKV:3 * D_RWKV]
    g_r = urs[:, 3 * D_RWKV:4 * D_RWKV]
    lora = urs[:, 4 * D_RWKV:P_RWKV]
    lane = lax.broadcasted_iota(jnp.int32, lora.shape, 1)
    lora_in = jnp.where(lane < LORA, jnp.tanh(lora), lora).astype(BF16)
    wa = _dg(lora_in, wl_ref[...])
    r_o[0, rs] = r
    v_o[0, rs] = v
    glu_o[0, rs] = glu_a * _sigmoid(glu_b)
    sgc_o[0, rs] = g_c * _sigmoid(g_c)
    yield

    z = -(w0_ref[...] + wa[:, :D_RWKV])
    softplus = jnp.maximum(z, 0.0) + jnp.log(1.0 + jnp.exp(-jnp.abs(z)))
    lw = -jnp.exp(-softplus - 0.5)
    a = _sigmoid(a0_ref[...] + wa[:, D_RWKV:])
    lw_o[0, rs] = lw
    bones = bones_ref[...]
    kkr = k * kkp_ref[...]
    ssq = _segsum(kkr * kkr, bones)
    yield

    kk = kkr * lax.rsqrt(jnp.maximum(ssq, 1e-24))
    kk_o[0, rs] = kk
    kka_o[0, rs] = kk * a
    km = k * (1.0 + (a - 1.0) * kap_ref[...])
    km_o[0, rs] = km
    rkk = _segsum(r * km * rkp_ref[...], bones)
    yield

    bon_o[0, rs] = rkk * v
    sgr_o[0, rs] = g_r * _sigmoid(g_r)


def _inproj_call(time_major, l, x, mod, s_shift, layer_params, bones, tile):
    if time_major:
        n_t, lag, _ = x.shape
        x = x.reshape(1, n_t * lag, D_MODEL)
        n_outer, n_inner, rows = 1, n_t * lag, n_t * lag
        grid = (1, 1)
        x_map = lambda i, t: (0, 0, 0)
        mod_spec = lambda j: pl.BlockSpec((None, lag, D_MODEL), lambda i, t: (l, 0, j))
        shift_spec = _layer_spec(s_shift, l, 2)
        hl_shape = (lag, D_MODEL)
        hl_spec = pl.BlockSpec(hl_shape, lambda i, t: (0, 0))
    else:
        n_outer, n_inner, _ = x.shape
        lag = 1
        rows = tile
        grid = (n_outer, n_inner // rows)
        x_map = lambda b, t: (b, t, 0)
        mod_spec = lambda j: pl.BlockSpec((None, None, 1, D_MODEL), lambda b, t: (l, b, 0, j))
        shift_spec = pl.BlockSpec((None, 1, D_MODEL), lambda b, t: (b, 0, 0))
        hl_shape = (n_outer, 1, D_MODEL)
        hl_spec = pl.BlockSpec((None, 1, D_MODEL), lambda b, t: (b, 0, 0))
    carry = pltpu.VMEM((-(-lag // SUBLANES) * SUBLANES + rows, P_RWKV), F32)
    out_block = pl.BlockSpec((1, rows, D_RWKV), x_map)
    out_sds = jax.ShapeDtypeStruct((n_outer, n_inner, D_RWKV), F32)
    return pl.pallas_call(
        functools.partial(_inproj_body, lag),
        grid=grid,
        in_specs=[pl.BlockSpec((1, rows, D_MODEL), x_map), mod_spec(0), mod_spec(1), shift_spec]
        + [_layer_spec(p, l, 2) for p in layer_params]
        + [pl.BlockSpec(bones.shape, lambda i, t: (0, 0))],
        out_specs=[out_block] * 10 + [hl_spec],
        out_shape=[out_sds] * 10 + [jax.ShapeDtypeStruct(hl_shape, F32)],
        scratch_shapes=[carry],
        compiler_params=pltpu.CompilerParams(
            dimension_semantics=("arbitrary", "arbitrary"), vmem_limit_bytes=VMEM_LIMIT),
        name="inproj_tm" if time_major else "inproj_bm",
    )(x, mod, mod, s_shift, *layer_params, bones)


CHUNK_SEQS = 8


def _wkv_chunk_body(r_ref, lw_ref, km_ref, v_ref, kk_ref, kka_ref, s0_ref, lt_ref,
                    y_ref, so_ref, s_scr):
    c = pl.program_id(1)
    n_seq = r_ref.shape[0]
    zeros_hh = jnp.zeros((HEAD_DIM, HEAD_DIM), F32)

    @pl.when(c == 0)
    def _():
        for i in range(n_seq):
            for p in range(N_PAIRS):
                top = jnp.concatenate([s0_ref[i, 2 * p], zeros_hh], axis=1)
                bot = jnp.concatenate([zeros_hh, s0_ref[i, 2 * p + 1]], axis=1)
                s_scr[i * N_PAIRS + p] = jnp.concatenate([top, bot], axis=0)

    lt = lt_ref[...]
    lane = lax.broadcasted_iota(jnp.int32, (CHUNK, LANES), 1)
    first = lane < HEAD_DIM
    row = lax.broadcasted_iota(jnp.int32, (LANES, LANES), 0)
    col = lax.broadcasted_iota(jnp.int32, (LANES, LANES), 1)
    same_head = (row < HEAD_DIM) == (col < HEAD_DIM)
    rt_i = row & (HEAD_DIM - 1)
    ct_i = col & (HEAD_DIM - 1)
    tri_strict = same_head & (ct_i < rt_i)
    tri_incl = same_head & (ct_i <= rt_i)
    eye = (row == col).astype(F32)

    def stack(xp):
        return jnp.concatenate([jnp.where(first, xp, 0.0), jnp.where(first, 0.0, xp)], axis=0)

    x2, y2, vs, hk, p_last = [], [], [], [], []
    for i in range(n_seq):
        lw = lw_ref[i]
        l_hi = lw.astype(BF16)
        l_r1 = lw - l_hi.astype(F32)
        l_mid = l_r1.astype(BF16)
        l_lo = (l_r1 - l_mid.astype(F32)).astype(BF16)
        cum = _dg(lt, l_hi) + (_dg(lt, l_mid) + _dg(lt, l_lo))
        cum_last = cum[CHUNK - 1:CHUNK, :]
        p_inv = jnp.exp(-cum)
        p_end = jnp.exp(cum_last - cum)
        at_all = -kk_ref[i] * jnp.exp(cum - lw)
        rt_all = r_ref[i] * jnp.exp(cum)
        kka = kka_ref[i]
        km = km_ref[i]
        bt_all = kka * p_inv
        kt_all = km * p_inv
        bh_all = kka * p_end
        kh_all = km * p_end
        v_all = v_ref[i]
        pl_all = jnp.exp(cum_last)
        for p in range(N_PAIRS):
            s = slice(LANES * p, LANES * (p + 1))
            x2.append(jnp.concatenate([stack(at_all[:, s]), stack(rt_all[:, s])], axis=0))
            y2.append(jnp.concatenate([stack(bt_all[:, s]), stack(kt_all[:, s])], axis=0))
            vs.append(stack(v_all[:, s]))
            hk.append(jnp.concatenate([stack(bh_all[:, s]), stack(kh_all[:, s])], axis=0))
            p_last.append(pl_all[:, s])

    units = range(n_seq * N_PAIRS)
    g = [_mm(x2[u], y2[u], _NT) for u in units]
    n_pow = [jnp.where(tri_strict, g[u][:LANES, :LANES], 0.0) for u in units]
    a_ak = [jnp.where(tri_strict, g[u][:LANES, LANES:], 0.0) for u in units]
    a_r = [jnp.concatenate([jnp.where(tri_incl, g[u][LANES:, :LANES], 0.0),
                            jnp.where(tri_incl, g[u][LANES:, LANES:], 0.0)], axis=1) for u in units]

    t_inv = [eye + n_pow[u] for u in units]
    span = 2
    while span < CHUNK:
        n_pow = [_mm(n_pow[u], n_pow[u], _NN) for u in units]
        t_inv = [t_inv[u] + _mm(t_inv[u], n_pow[u], _NN) for u in units]
        span *= 2

    s_bd = [s_scr[u] for u in units]
    xs = [_mm(x2[u], s_bd[u], _NT) for u in units]
    akv = [_mm(a_ak[u], vs[u], _NN) for u in units]
    z_mat = [_mm(t_inv[u], xs[u][:LANES] + akv[u], _NN) for u in units]
    zv = [jnp.concatenate([z_mat[u], vs[u]], axis=0) for u in units]
    y_bd = [xs[u][LANES:] + _mm(a_r[u], zv[u], _NN) for u in units]
    s_new = [s_bd[u] * p_last[u] + _mm(zv[u], hk[u], _TN) for u in units]
    for u in units:
        i, p = divmod(u, N_PAIRS)
        y_ref[i, :, LANES * p:LANES * (p + 1)] = y_bd[u][:HEAD_DIM] + y_bd[u][HEAD_DIM:]
        s_scr[u] = s_new[u]
        so_ref[i, 2 * p] = s_new[u][:HEAD_DIM, :HEAD_DIM]
        so_ref[i, 2 * p + 1] = s_new[u][HEAD_DIM:, HEAD_DIM:]


def _wkv_chunk_call(r, lw, km, v, kk, kka, s0, lt):
    n_b, n_t, _ = r.shape
    nq = CHUNK_SEQS
    seq_block = pl.BlockSpec((nq, CHUNK, D_RWKV), lambda b, c: (b, c, 0))
    st_block = pl.BlockSpec((nq, N_HEADS, HEAD_DIM, HEAD_DIM), lambda b, c: (b, 0, 0, 0))
    return pl.pallas_call(
        _wkv_chunk_body,
        grid=(n_b // nq, n_t // CHUNK),
        in_specs=[seq_block] * 6 + [st_block, pl.BlockSpec(lt.shape, lambda b, c: (0, 0))],
        out_specs=[seq_block, st_block],
        out_shape=[jax.ShapeDtypeStruct(r.shape, F32), jax.ShapeDtypeStruct(s0.shape, F32)],
        scratch_shapes=[pltpu.VMEM((nq * N_PAIRS, LANES, LANES), F32)],
        compiler_params=pltpu.CompilerParams(
            dimension_semantics=("arbitrary", "arbitrary"), vmem_limit_bytes=VMEM_LIMIT),
        name="wkv_chunk",
    )(r, lw, km, v, kk, kka, s0, lt)


SHORT_T = 4
SHORT_SEQS = LANES // (2 * SHORT_T)


def _wkv_short_body(layer, r_ref, lw_ref, km_ref, v_ref, kk_ref, kka_ref, s_all_ref, *rest):
    y_ref, so_all_ref = rest[-2:]
    if layer is None:
        s_ref, so_ref = s_all_ref, so_all_ref
    else:
        s_ref, so_ref = s_all_ref.at[layer], so_all_ref.at[layer]
        for other in range(s_all_ref.shape[0]):
            if other != layer:
                so_all_ref[other] = s_all_ref[other]
    n_seq = s_ref.shape[0]
    grp = 2 * SHORT_T
    row = lax.broadcasted_iota(jnp.int32, (LANES, LANES), 0)
    col = lax.broadcasted_iota(jnp.int32, (LANES, LANES), 1)
    same_blk = (row // SHORT_T) == (col // SHORT_T)
    t_row = row % SHORT_T
    t_col = col % SHORT_T
    strict = same_blk & (t_col < t_row)
    incl = same_blk & (t_col <= t_row)
    eye = (row == col).astype(F32)
    sum_mat = jnp.concatenate([incl.astype(BF16), same_blk.astype(BF16)], axis=0)
    own = ((row // SHORT_T) % 2) == (col // HEAD_DIM)
    zeros_hh = jnp.zeros((HEAD_DIM, HEAD_DIM), F32)

    def rows_of(ref, p):
        x = ref[:, :, LANES * p:LANES * (p + 1)]
        return jnp.concatenate([x, x], axis=1).reshape(n_seq * grp, LANES)

    def mask(x):
        return jnp.where(own, x, 0.0)

    pairs = range(N_PAIRS)
    x2, y2, vs, bh, kh, p_last = [], [], [], [], [], []
    for p in pairs:
        lw = rows_of(lw_ref, p)
        l_hi = lw.astype(BF16)
        l_r1 = lw - l_hi.astype(F32)
        l_mid = l_r1.astype(BF16)
        l_lo = (l_r1 - l_mid.astype(F32)).astype(BF16)
        sums = _dg(sum_mat, l_hi) + (_dg(sum_mat, l_mid) + _dg(sum_mat, l_lo))
        cum = sums[:LANES]
        tot = sums[LANES:]
        p_inv = jnp.exp(-cum)
        p_end = jnp.exp(tot - cum)
        kka = rows_of(kka_ref, p)
        km = rows_of(km_ref, p)
        x2.append(jnp.concatenate([mask(-rows_of(kk_ref, p) * jnp.exp(cum - lw)),
                                   mask(rows_of(r_ref, p) * jnp.exp(cum))], axis=0))
        y2.append(jnp.concatenate([mask(kka * p_inv), mask(km * p_inv)], axis=0))
        vs.append(mask(rows_of(v_ref, p)))
        bh.append(mask(kka * p_end))
        kh.append(mask(km * p_end))
        p_last.append(jnp.exp(tot))

    g = [_mm(x2[p], y2[p], _NT) for p in pairs]
    n_mat = [jnp.where(strict, g[p][:LANES, :LANES], 0.0) for p in pairs]
    a_ak = [jnp.where(strict, g[p][:LANES, LANES:], 0.0) for p in pairs]
    a_r = [jnp.concatenate([jnp.where(incl, g[p][LANES:, :LANES], 0.0),
                            jnp.where(incl, g[p][LANES:, LANES:], 0.0)], axis=1) for p in pairs]
    n_sq = [_mm(n_mat[p], n_mat[p], _NN) for p in pairs]
    t_inv = [(eye + n_mat[p]) + _mm(eye + n_mat[p], n_sq[p], _NN) for p in pairs]
    akv = [_mm(a_ak[p], vs[p], _NN) for p in pairs]

    def seq_rows(x, i):
        return x[grp * i:grp * (i + 1)]

    s_bd, xa_s, xr_s = [], [], []
    for p in pairs:
        s_p, xa_p, xr_p = [], [], []
        for i in range(n_seq):
            top = jnp.concatenate([s_ref[i, 2 * p], zeros_hh], axis=1)
            bot = jnp.concatenate([zeros_hh, s_ref[i, 2 * p + 1]], axis=1)
            s_i = jnp.concatenate([top, bot], axis=0)
            x_i = jnp.concatenate([seq_rows(x2[p][:LANES], i), seq_rows(x2[p][LANES:], i)], axis=0)
            xs_i = _mm(x_i, s_i, _NT)
            s_p.append(s_i)
            xa_p.append(xs_i[:grp])
            xr_p.append(xs_i[grp:])
        s_bd.append(s_p)
        xa_s.append(jnp.concatenate(xa_p, axis=0))
        xr_s.append(jnp.concatenate(xr_p, axis=0))

    z_mat = [_mm(t_inv[p], xa_s[p] + akv[p], _NN) for p in pairs]
    zv = [jnp.concatenate([z_mat[p], vs[p]], axis=0) for p in pairs]
    y_bd = [xr_s[p] + _mm(a_r[p], zv[p], _NN) for p in pairs]
    for p in pairs:
        y3 = y_bd[p].reshape(n_seq, grp, LANES)
        y_ref[:, :, LANES * p:LANES * (p + 1)] = y3[:, :SHORT_T] + y3[:, SHORT_T:]
        for i in range(n_seq):
            zv_i = jnp.concatenate([seq_rows(z_mat[p], i), seq_rows(vs[p], i)], axis=0)
            hk_i = jnp.concatenate([seq_rows(bh[p], i), seq_rows(kh[p], i)], axis=0)
            s_new = (s_bd[p][i] * p_last[p][grp * i:grp * i + 1]
                     + _mm(zv_i, hk_i, _TN))
            so_ref[i, 2 * p] = s_new[:HEAD_DIM, :HEAD_DIM]
            so_ref[i, 2 * p + 1] = s_new[HEAD_DIM:, HEAD_DIM:]


def _wkv_short_call(l, r, lw, km, v, kk, kka, s_all, s_new_all):
    n_b = r.shape[0]
    nq = SHORT_SEQS
    seq_block = pl.BlockSpec((nq, SHORT_T, D_RWKV), lambda i: (i, 0, 0))
    carried = [] if s_new_all is None else [s_new_all]
    if carried:
        st_block = pl.BlockSpec((None, nq, N_HEADS, HEAD_DIM, HEAD_DIM), lambda i: (l, i, 0, 0, 0))
    else:
        st_block = pl.BlockSpec((s_all.shape[0], nq, N_HEADS, HEAD_DIM, HEAD_DIM),
                                lambda i: (0, i, 0, 0, 0))
    return pl.pallas_call(
        functools.partial(_wkv_short_body, None if carried else l),
        grid=(n_b // nq,),
        in_specs=[seq_block] * 6 + [st_block] + [pl.BlockSpec(memory_space=pl.ANY)] * len(carried),
        out_specs=[seq_block, st_block],
        out_shape=[jax.ShapeDtypeStruct(r.shape, F32), jax.ShapeDtypeStruct(s_all.shape, F32)],
        input_output_aliases={7: 1} if carried else {},
        compiler_params=pltpu.CompilerParams(
            dimension_semantics=("arbitrary",), vmem_limit_bytes=VMEM_LIMIT),
        name="wkv_short",
    )(r, lw, km, v, kk, kka, s_all, *carried)


CONV_ROW_BLOCK = 256
HIST_PAD = 32


def _merge_and_residual(y, bonus, sgr, conv, sgc, x, gate, gnr_g, gnr_b, gnc_g, gnc_b,
                        wout_ref, gpost, bones):
    y_r = (_group_norm(y, gnr_g, gnr_b, RWKV_GN_EPS, bones) + bonus) * sgr
    cn = _group_norm(conv, gnc_g, gnc_b, CONV_GN_EPS, bones)
    y_c = cn * _sigmoid(cn) * sgc
    mix = _dg(y_r.astype(BF16), wout_ref[0:D_RWKV, :]) + _dg(y_c.astype(BF16), wout_ref[D_RWKV:, :])
    ms = jnp.mean(mix * mix, axis=-1, keepdims=True)
    return x + gate * (mix * lax.rsqrt(ms + RMS_EPS) * gpost)


def _post_bm_body(y_ref, bon_ref, sgr_ref, glu_ref, sgc_ref, x_ref, gate_ref, sconv_ref,
                  gnrg_ref, gnrb_ref, wdw_ref, bdw_ref, gncg_ref, gncb_ref, wout_ref, gpost_ref,
                  bones_ref, xo_ref, buf, conv):
    step = pl.program_id(1)
    rows = y_ref.shape[1]
    off = HIST_PAD - HIST
    shifted_rows = rows + HIST_PAD - SUBLANES

    @pl.when(step == 0)
    def _():
        buf[0, 0:off, :] = jnp.zeros((off, D_CONV), F32)
        buf[0, off:HIST_PAD, :] = sconv_ref[0]

    buf[0, HIST_PAD:HIST_PAD + rows, :] = glu_ref[0]
    for q in range(1, SUBLANES):
        buf[q, 0:shifted_rows, :] = buf[0, q:q + shifted_rows, :]
    for r0 in range(0, rows, CONV_ROW_BLOCK):
        acc = jnp.broadcast_to(bdw_ref[...], (CONV_ROW_BLOCK, D_CONV))
        for j in range(CONV_W):
            q = (j + off) % SUBLANES
            base = r0 + j + off - q
            acc = acc + wdw_ref[j:j + 1, :] * buf[q, base:base + CONV_ROW_BLOCK, :]
        conv[r0:r0 + CONV_ROW_BLOCK, :] = acc
    buf[0, off:HIST_PAD, :] = buf[0, rows + off:rows + HIST_PAD, :]

    xo_ref[0] = _merge_and_residual(
        y_ref[0], bon_ref[0], sgr_ref[0], conv[...], sgc_ref[0], x_ref[0], gate_ref[...],
        gnrg_ref[...], gnrb_ref[...], gncg_ref[...], gncb_ref[...], wout_ref, gpost_ref[...],
        bones_ref[...])


def _post_tm_body(y_ref, bon_ref, sgr_ref, glu_ref, sgc_ref, x_ref, gate_ref, sconv_ref,
                  gnrg_ref, gnrb_ref, wdw_ref, bdw_ref, gncg_ref, gncb_ref, wout_ref, gpost_ref,
                  bones_ref, xo_ref, co_ref):
    n_t, rows, _ = y_ref.shape
    for i in range(n_t, n_t + HIST):
        co_ref[i - n_t] = sconv_ref[i] if i < HIST else glu_ref[i - HIST]
    convs = []
    for t in range(n_t):
        acc = jnp.broadcast_to(bdw_ref[...], (rows, D_CONV))
        for j in range(CONV_W):
            i = t + j
            src = sconv_ref[i] if i < HIST else glu_ref[i - HIST]
            acc = acc + wdw_ref[j:j + 1, :] * src
        convs.append(acc)
    flat = lambda ref: ref[...].reshape(n_t * rows, ref.shape[-1])
    out = _merge_and_residual(
        flat(y_ref), flat(bon_ref), flat(sgr_ref), jnp.concatenate(convs, axis=0), flat(sgc_ref),
        flat(x_ref), jnp.concatenate([gate_ref[...]] * n_t, axis=0),
        gnrg_ref[...], gnrb_ref[...], gncg_ref[...], gncb_ref[...], wout_ref, gpost_ref[...],
        bones_ref[...])
    xo_ref[...] = out.reshape(n_t, rows, D_MODEL)


def _post_call(time_major, l, y, bonus, sgr, glu, sgc, x, mod, s_conv, layer_params, bones, tile):
    n_outer, n_inner, _ = x.shape
    if time_major:
        rows = n_inner
        grid = (1, 1)
        lead = n_outer
        x_map = lambda i, t: (0, 0, 0)
        gate_spec = pl.BlockSpec((None, rows, D_MODEL), lambda i, t: (l, 0, 2))
        sconv_spec = _layer_spec(s_conv, l, 2)
        scratch = []
        body = _post_tm_body
    else:
        rows = tile
        grid = (n_outer, n_inner // rows)
        lead = 1
        x_map = lambda b, t: (b, t, 0)
        gate_spec = pl.BlockSpec((None, None, 1, D_MODEL), lambda b, t: (l, b, 0, 2))
        sconv_spec = pl.BlockSpec((1, HIST, D_CONV), lambda b, t: (b, 0, 0))
        scratch = [pltpu.VMEM((SUBLANES, HIST_PAD + rows, D_CONV), F32),
                   pltpu.VMEM((rows, D_CONV), F32)]
        body = _post_bm_body
    half = pl.BlockSpec((lead, rows, D_RWKV), x_map)
    full = pl.BlockSpec((lead, rows, D_MODEL), x_map)
    out_specs, out_shape = full, jax.ShapeDtypeStruct(x.shape, F32)
    if time_major:
        out_specs = [full, pl.BlockSpec(s_conv.shape[1:], lambda i, t: (0, 0, 0))]
        out_shape = [out_shape, jax.ShapeDtypeStruct(s_conv.shape[1:], F32)]
    return pl.pallas_call(
        body,
        grid=grid,
        in_specs=[half] * 5 + [full, gate_spec, sconv_spec]
        + [_layer_spec(p, l, 2) for p in layer_params]
        + [pl.BlockSpec(bones.shape, lambda i, t: (0, 0))],
        out_specs=out_specs,
        out_shape=out_shape,
        scratch_shapes=scratch,
        compiler_params=pltpu.CompilerParams(
            dimension_semantics=("arbitrary", "arbitrary"), vmem_limit_bytes=VMEM_LIMIT),
        name="post_tm" if time_major else "post_bm",
    )(y, bonus, sgr, glu, sgc, x, mod, s_conv, *layer_params, bones)


POST_TILE = 512


def _block_ones():
    i = lax.broadcasted_iota(jnp.int32, (MXU_DIM, MXU_DIM), 0) // HEAD_DIM
    j = lax.broadcasted_iota(jnp.int32, (MXU_DIM, MXU_DIM), 1) // HEAD_DIM
    return (i == j).astype(BF16)


def _lower_tri():
    i = lax.broadcasted_iota(jnp.int32, (CHUNK, CHUNK), 0)
    j = lax.broadcasted_iota(jnp.int32, (CHUNK, CHUNK), 1)
    return (j <= i).astype(BF16)


def kernel(x_prompt, x_sample, c_prompt, c_sample, state_shift, state_wkv, state_conv, w_ada, b_ada, g_pre, g_post, w_in, mu, w0, w_up, a0, a_up, k_k, k_a, r_k, gn_r_g, gn_r_b, w_dw, b_dw, gn_c_g, gn_c_b, w_out):
    depth = w_in.shape[0]
    n_p, t_p, _ = x_prompt.shape
    n_s, t_s, _ = x_sample.shape
    bones = _block_ones()
    lt = _lower_tri()

    mod_p, mod_s = _mod_call(c_prompt, c_sample, w_ada, b_ada)
    mod_p = mod_p[:, :, None, :]

    xp = x_prompt
    xs = jnp.transpose(x_sample, (1, 0, 2))
    zeros_shift = jnp.zeros((n_p, 1, D_MODEL), F32)
    zeros_wkv = jnp.zeros((n_p, N_HEADS, HEAD_DIM, HEAD_DIM), F32)
    zeros_conv = jnp.zeros((n_p, HIST, D_CONV), F32)
    sconv_tm = jnp.transpose(state_conv, (0, 2, 1, 3))

    seq_major = lambda a: jnp.transpose(a, (1, 0, 2))

    rows = lambda a: a.reshape(depth, 1, -1)
    zl = jnp.zeros((depth, LORA, D_RWKV), F32)
    w_lora = jnp.concatenate(
        [jnp.concatenate([w_up, zl], axis=2), jnp.concatenate([zl, a_up], axis=2)],
        axis=1).astype(BF16)
    in_params = (rows(g_pre), w_in.astype(BF16), rows(mu), rows(w0), rows(a0), w_lora,
                 rows(k_k), rows(k_a), rows(r_k))
    post_params = (rows(gn_r_g), rows(gn_r_b), w_dw, rows(b_dw), rows(gn_c_g), rows(gn_c_b),
                   w_out.astype(BF16), rows(g_post))

    shifts_p, wkvs_p, convs_p, shifts_s, convs_s = [], [], [], [], []
    wkv_s = None
    for l in range(depth):
        (r, lw, km, v, kk, kka, bonus, sgr, glu, sgc, hl) = _inproj_call(
            False, l, xp, mod_p, zeros_shift, in_params, bones, INPROJ_TILE)
        y, s_new = _wkv_chunk_call(r, lw, km, v, kk, kka, zeros_wkv, lt)
        xp = _post_call(False, l, y, bonus, sgr, glu, sgc, xp, mod_p, zeros_conv, post_params,
                        bones, POST_TILE)
        shifts_p.append(hl[:, 0])
        wkvs_p.append(s_new)
        convs_p.append(glu[:, -HIST:] if t_p >= HIST
                       else jnp.concatenate([zeros_conv, glu], axis=1)[:, -HIST:])

        *seq_outs, hl = _inproj_call(True, l, xs, mod_s, state_shift, in_params, bones, None)
        (r, lw, km, v, kk, kka, bonus, sgr, glu, sgc) = (
            a.reshape(t_s, n_s, D_RWKV) for a in seq_outs)
        y, wkv_s = _wkv_short_call(l, *(seq_major(a) for a in (r, lw, km, v, kk, kka)),
                                   state_wkv, wkv_s)
        y = seq_major(y)
        xs, conv_new = _post_call(True, l, y, bonus, sgr, glu, sgc, xs, mod_s, sconv_tm,
                                  post_params, bones, None)
        shifts_s.append(hl)
        convs_s.append(conv_new)

    return (xp, jnp.transpose(xs, (1, 0, 2)),
            jnp.stack(shifts_p), jnp.stack(wkvs_p), jnp.stack(convs_p),
            jnp.stack(shifts_s), wkv_s, jnp.transpose(jnp.stack(convs_s), (0, 2, 1, 3)))
```
